```python
import math
import jax, jax.numpy as jnp
from jax import lax
import numpy as np

D_MODEL = 1024
BATCH = 8
SEQ = 2048
DEPTH = 1

HEAD_DIM = 64
D_DIFF = D_MODEL // 2
D_DIL = D_MODEL - D_DIFF
DIFF_V_DIM = 2 * HEAD_DIM
N_HEADS_DIFF = D_DIFF // DIFF_V_DIM
N_HEADS_DIL = D_DIL // HEAD_DIM
D_MIX = D_DIFF + D_DIL
D_IN = 3 * D_DIFF + 3 * D_DIL
DIL_PATTERNS = ((128, 1), (512, 4), (2048, 16))
DIL_BLOCK = 64
Q_BLOCK = 128
D_FF = 4 * D_MODEL
D_PLE = 256
ROPE_THETA = 10000.0
NORM_EPS = 1e-6
SUBLN_EPS = 1e-5
NEG_INF = -1e30

kernel_name = "hybrid_diffattn_dilated_encoder_layer"


def rms_norm(x, g, eps=NORM_EPS):
    xf = x.astype(jnp.float32)
    y = xf * lax.rsqrt(jnp.mean(xf * xf, axis=-1, keepdims=True) + eps)
    return (y * g.astype(jnp.float32)).astype(x.dtype)


def rope_tables(seq):
    inv = ROPE_THETA ** (-jnp.arange(0, HEAD_DIM, 2, dtype=jnp.float32) / HEAD_DIM)
    ang = jnp.arange(seq, dtype=jnp.float32)[:, None] * inv[None, :]
    return jnp.cos(ang), jnp.sin(ang)


def apply_rope(t, cos, sin):
    t1, t2 = jnp.split(t.astype(jnp.float32), 2, axis=-1)
    out = jnp.concatenate([t1 * cos - t2 * sin, t1 * sin + t2 * cos], axis=-1)
    return out.astype(t.dtype)


def diff_attention(q, k, v, lam, subln_g, lam_init):
    B, H, _, S, Dh = q.shape
    nq = S // Q_BLOCK
    scale = Dh ** -0.5
    qb = q.reshape(B, H, 2, nq, Q_BLOCK, Dh).transpose(3, 0, 1, 2, 4, 5)
    kf = k.astype(jnp.float32)
    vf = v.astype(jnp.float32)

    def block(qblk):
        s = jnp.einsum('bhmqd,bhmkd->bhmqk', qblk.astype(jnp.float32) * scale, kf)
        pr = jax.nn.softmax(s, axis=-1)
        a = pr[:, :, 0] - lam * pr[:, :, 1]
        return jnp.einsum('bhqk,bhkd->bhqd', a, vf)

    o = lax.map(block, qb)
    o = o.transpose(1, 2, 0, 3, 4).reshape(B, H, S, 2 * Dh)
    o = rms_norm(o, subln_g, SUBLN_EPS) * (1.0 - lam_init)
    return o


def dilated_branch(q, k, v, dilation, radius):
    B, H, S, Dh = q.shape
    L = S // dilation

    def split(t):
        return t.reshape(B, H, L, dilation, Dh).transpose(0, 1, 3, 2, 4)

    qs, ks, vs = split(q), split(k), split(v)
    nb = -(-L // DIL_BLOCK)
    Lp = nb * DIL_BLOCK
    qb = jnp.pad(qs, ((0, 0), (0, 0), (0, 0), (0, Lp - L), (0, 0)))
    qb = qb.reshape(B, H, dilation, nb, DIL_BLOCK, Dh)

    def windows(t):
        tp = jnp.pad(t, ((0, 0), (0, 0), (0, 0), (DIL_BLOCK, Lp - L + DIL_BLOCK), (0, 0)))
        tb = tp.reshape(B, H, dilation, nb + 2, DIL_BLOCK, Dh)
        return jnp.concatenate([tb[:, :, :, :-2], tb[:, :, :, 1:-1], tb[:, :, :, 2:]], axis=-2)

    kw, vw = windows(ks), windows(vs)
    s = jnp.einsum('bhrnqd,bhrnkd->bhrnqk', qb.astype(jnp.float32),
                   kw.astype(jnp.float32)) * (Dh ** -0.5)
    a = jnp.arange(DIL_BLOCK)[:, None]
    c = jnp.arange(3 * DIL_BLOCK)[None, :]
    off = c - DIL_BLOCK - a
    kpos = (jnp.arange(nb)[:, None, None] - 1) * DIL_BLOCK + c[None]
    mask = (jnp.abs(off)[None] <= radius) & (kpos >= 0) & (kpos < L)
    s = jnp.where(mask, s, NEG_INF)
    lse = jax.nn.logsumexp(s, axis=-1)
    pr = jnp.exp(s - lse[..., None])
    o = jnp.einsum('bhrnqk,bhrnkd->bhrnqd', pr, vw.astype(jnp.float32))
    o = o.reshape(B, H, dilation, Lp, Dh)[:, :, :, :L]
    lse = lse.reshape(B, H, dilation, Lp)[:, :, :, :L]
    o = o.transpose(0, 1, 3, 2, 4).reshape(B, H, S, Dh)
    lse = lse.transpose(0, 1, 3, 2).reshape(B, H, S)
    return o, lse


def dilated_attention(q, k, v):
    outs, lses = [], []
    for window, dilation in DIL_PATTERNS:
        o, l = dilated_branch(q, k, v, dilation, window // (2 * dilation))
        outs.append(o)
        lses.append(l)
    w = jax.nn.softmax(jnp.stack(lses, axis=0), axis=0)
    return jnp.sum(w[..., None] * jnp.stack(outs, axis=0), axis=0)


def setup_inputs(seed: int = 0) -> dict:
    key = jax.random.key(seed)
    ks = jax.random.split(key, 20)
    f32 = jnp.float32
    nrm = lambda k, shape, s: jax.random.normal(k, shape, f32) * s
    gain = lambda k, shape: 1.0 + 0.05 * jax.random.normal(k, shape, f32)
    return {
        "x": jax.random.normal(ks[0], (BATCH, SEQ, D_MODEL), f32),
        "p": jax.random.normal(ks[1], (DEPTH, BATCH, SEQ, D_PLE), f32),
        "w_in": nrm(ks[2], (DEPTH, D_MODEL, D_IN), D_MODEL ** -0.5),
        "w_out": nrm(ks[3], (DEPTH, D_MIX, D_MODEL), D_MIX ** -0.5),
        "g_mix": gain(ks[4], (DEPTH, D_MODEL)),
        "lambda_q1": nrm(ks[5], (DEPTH, HEAD_DIM), 0.1),
        "lambda_k1": nrm(ks[6], (DEPTH, HEAD_DIM), 0.1),
        "lambda_q2": nrm(ks[7], (DEPTH, HEAD_DIM), 0.1),
        "lambda_k2": nrm(ks[8], (DEPTH, HEAD_DIM), 0.1),
        "g_subln": gain(ks[9], (DEPTH, DIFF_V_DIM)),
        "g_mlp": gain(ks[10], (DEPTH, D_MODEL)),
        "w_up": nrm(ks[11], (DEPTH, D_MODEL, D_FF), D_MODEL ** -0.5),
        "w_down": nrm(ks[12], (DEPTH, D_FF, D_MODEL), 0.5 * D_FF ** -0.5),
        "g_ple": gain(ks[13], (DEPTH, D_MODEL)),
        "w_ple_gate": nrm(ks[14], (DEPTH, D_MODEL, D_MODEL), D_MODEL ** -0.5),
        "w_ple_proj": nrm(ks[15], (DEPTH, D_PLE, D_MODEL), D_PLE ** -0.5),
        "g_final": gain(ks[16], (D_MODEL,)),
    }


def reference(x, p, w_in, w_out, g_mix, lambda_q1, lambda_k1, lambda_q2, lambda_k2, g_subln,
              g_mlp, w_up, w_down, g_ple, w_ple_gate, w_ple_proj, g_final):
    B, S, _ = x.shape
    cos, sin = rope_tables(S)
    splits = [D_DIFF, 2 * D_DIFF, 3 * D_DIFF, 3 * D_DIFF + D_DIL, 3 * D_DIFF + 2 * D_DIL]
    h = x
    for i in range(DEPTH):
        hn = rms_norm(h, g_mix[i])
        proj = hn @ w_in[i]
        qd, kd, vd, qg, kg, vg = jnp.split(proj, splits, axis=-1)

        qd = apply_rope(qd.reshape(B, S, N_HEADS_DIFF, 2, HEAD_DIM).transpose(0, 2, 3, 1, 4), cos, sin)
        kd = apply_rope(kd.reshape(B, S, N_HEADS_DIFF, 2, HEAD_DIM).transpose(0, 2, 3, 1, 4), cos, sin)
        vd = vd.reshape(B, S, N_HEADS_DIFF, DIFF_V_DIM).transpose(0, 2, 1, 3)
        lam_init = 0.8 - 0.6 * math.exp(-0.3 * i)
        lam = (jnp.exp(jnp.sum(lambda_q1[i].astype(jnp.float32) * lambda_k1[i].astype(jnp.float32)))
               - jnp.exp(jnp.sum(lambda_q2[i].astype(jnp.float32) * lambda_k2[i].astype(jnp.float32)))
               + lam_init)
        od = diff_attention(qd, kd, vd, lam, g_subln[i], lam_init)
        od = od.transpose(0, 2, 1, 3).reshape(B, S, D_DIFF)

        qg = apply_rope(qg.reshape(B, S, N_HEADS_DIL, HEAD_DIM).transpose(0, 2, 1, 3), cos, sin)
        kg = apply_rope(kg.reshape(B, S, N_HEADS_DIL, HEAD_DIM).transpose(0, 2, 1, 3), cos, sin)
        vg = vg.reshape(B, S, N_HEADS_DIL, HEAD_DIM).transpose(0, 2, 1, 3)
        og = dilated_attention(qg, kg, vg)
        og = og.transpose(0, 2, 1, 3).reshape(B, S, D_DIL)

        mixed = jnp.concatenate([od, og], axis=-1).astype(h.dtype)
        h = h + mixed @ w_out[i]

        hn = rms_norm(h, g_mlp[i])
        h = h + jnp.square(jax.nn.relu(hn @ w_up[i])) @ w_down[i]

        gate = jax.nn.sigmoid(rms_norm(h, g_ple[i]) @ w_ple_gate[i])
        h = h + gate * (p[i] @ w_ple_proj[i])
    return rms_norm(h, g_final)
```

```python
import functools
import math

import jax
import jax.numpy as jnp
from jax import lax
from jax.experimental import pallas as pl
from jax.experimental.pallas import tpu as pltpu

D_MODEL = 1024
HEAD_DIM = 64
D_DIFF = 512
D_DIL = 512
D_GROUP = 3 * D_DIFF
N_HEADS_DIFF = 4
DIL_PATTERNS = ((128, 1), (512, 4), (2048, 16))
D_FF = 4 * D_MODEL
D_PLE = 256
ROPE_THETA = 10000.0
NORM_EPS = 1e-6
SUBLN_EPS = 1e-5
NEG_INF = -1e30
LAM_INIT = 0.8 - 0.6 * math.exp(-0.3 * 0)

LANES = 128
VMEM_LIMIT_BYTES = 56 * 1024 * 1024

PROJ_ROWS = 512
DIFF_Q_ROWS = 256
DIL_Q_ROWS = 128
POST_ROWS = 512
FF_CHUNK = 1024

BF16 = jnp.bfloat16
F32 = jnp.float32
NT_DIMS = (((1,), (1,)), ((), ()))


def _rms(x, g, eps):
    return x * lax.rsqrt(jnp.mean(x * x, axis=-1, keepdims=True) + eps) * g


def _proj_kernel(x_ref, g_ref, w_ref, cos_ref, sin_ref, diff_ref, dil_ref):
    x = x_ref[0]
    hn = _rms(x, g_ref[...], NORM_EPS).astype(BF16)
    cos = cos_ref[...]
    sin = sin_ref[...]
    lane = lax.broadcasted_iota(jnp.int32, cos.shape, 1)
    first_half = (lane % HEAD_DIM) < (HEAD_DIM // 2)

    def rope(t):
        swapped = jnp.where(first_half,
                            pltpu.roll(t, LANES - HEAD_DIM // 2, 1),
                            pltpu.roll(t, HEAD_DIM // 2, 1))
        return t * cos + swapped * sin

    scale = HEAD_DIM ** -0.5
    for grp, out_ref in ((0, diff_ref), (1, dil_ref)):
        for part in range(3):
            col0 = grp * D_GROUP + part * D_DIFF
            acc = jnp.dot(hn, w_ref[:, col0:col0 + D_DIFF], preferred_element_type=F32)
            for c in range(D_DIFF // LANES):
                t = acc[:, c * LANES:(c + 1) * LANES]
                if part < 2:
                    t = rope(t)
                if part == 0:
                    t = t * scale
                o0 = part * D_DIFF + c * LANES
                out_ref[0, :, o0:o0 + LANES] = t.astype(out_ref.dtype)


def _project(x, g_mix, w_in, cos_t, sin_t):
    B, S, D = x.shape
    rows = PROJ_ROWS
    return pl.pallas_call(
        _proj_kernel,
        grid=(B, S // rows),
        in_specs=[
            pl.BlockSpec((1, rows, D), lambda b, i: (b, i, 0)),
            pl.BlockSpec((1, D), lambda b, i: (0, 0)),
            pl.BlockSpec((D, 2 * D_GROUP), lambda b, i: (0, 0), pipeline_mode=pl.Buffered(1)),
            pl.BlockSpec((rows, LANES), lambda b, i: (i, 0)),
            pl.BlockSpec((rows, LANES), lambda b, i: (i, 0)),
        ],
        out_specs=[
            pl.BlockSpec((1, rows, D_GROUP), lambda b, i: (b, i, 0)),
            pl.BlockSpec((1, rows, D_GROUP), lambda b, i: (b, i, 0)),
        ],
        out_shape=[
            jax.ShapeDtypeStruct((B, S, D_GROUP), BF16),
            jax.ShapeDtypeStruct((B, S, D_GROUP), F32),
        ],
        compiler_params=pltpu.CompilerParams(
            dimension_semantics=("parallel", "parallel"),
            vmem_limit_bytes=VMEM_LIMIT_BYTES),
        name="proj_rope",
    )(x, g_mix, w_in, cos_t, sin_t)


def _diff_kernel(q_ref, k_ref, v_ref, lq1_ref, lk1_ref, lq2_ref, lk2_ref, g_ref, o_ref):
    q = q_ref[0]
    k = k_ref[0]
    v = v_ref[0]
    lane = lax.broadcasted_iota(jnp.int32, q.shape, 1)
    outs = []
    for m in range(2):
        in_map = (lane >= m * HEAD_DIM) & (lane < (m + 1) * HEAD_DIM)
        qm = jnp.where(in_map, q, jnp.zeros_like(q))
        s = lax.dot_general(qm, k, NT_DIMS, preferred_element_type=F32)
        mx = jnp.max(s, axis=-1, keepdims=True)
        p = jnp.exp(s - mx)
        z = jnp.sum(p, axis=-1, keepdims=True)
        o = jnp.dot(p.astype(BF16), v, preferred_element_type=F32)
        outs.append(o / z)
    lam = (jnp.exp(jnp.sum(lq1_ref[...] * lk1_ref[...], axis=-1, keepdims=True))
           - jnp.exp(jnp.sum(lq2_ref[...] * lk2_ref[...], axis=-1, keepdims=True))
           + LAM_INIT)
    o = outs[0] - lam * outs[1]
    o = _rms(o, g_ref[...], SUBLN_EPS) * (1.0 - LAM_INIT)
    o_ref[0] = o.astype(o_ref.dtype)


def _diff_attention(diff_qkv, lq1, lk1, lq2, lk2, g_subln):
    B, S, _ = diff_qkv.shape
    rows = DIFF_Q_ROWS
    nblk = D_DIFF // LANES
    small = lambda n: pl.BlockSpec((1, n), lambda b, h, i: (0, 0))
    return pl.pallas_call(
        _diff_kernel,
        grid=(B, N_HEADS_DIFF, S // rows),
        in_specs=[
            pl.BlockSpec((1, rows, LANES), lambda b, h, i: (b, i, h)),
            pl.BlockSpec((1, S, LANES), lambda b, h, i: (b, 0, nblk + h)),
            pl.BlockSpec((1, S, LANES), lambda b, h, i: (b, 0, 2 * nblk + h)),
            small(HEAD_DIM), small(HEAD_DIM), small(HEAD_DIM), small(HEAD_DIM),
            small(2 * HEAD_DIM),
        ],
        out_specs=pl.BlockSpec((1, rows, LANES), lambda b, h, i: (b, i, h)),
        out_shape=jax.ShapeDtypeStruct((B, S, D_DIFF), BF16),
        compiler_params=pltpu.CompilerParams(
            dimension_semantics=("parallel", "parallel", "arbitrary"),
            vmem_limit_bytes=VMEM_LIMIT_BYTES),
        name="diff_attn",
    )(diff_qkv, diff_qkv, diff_qkv, lq1, lk1, lq2, lk2, g_subln)


def _dil_kernel(q_ref, k_ref, v_ref, o_ref, n_scr, m_scr, z_scr):
    S = q_ref.shape[1]
    rows = DIL_Q_ROWS
    lane = lax.broadcasted_iota(jnp.int32, (rows, LANES), 1)
    head0 = lane < HEAD_DIM

    for pi, (window, d) in enumerate(DIL_PATTERNS):
        radius = window // (2 * d)
        L = S // d
        wk = min(rows + 2 * radius, L)
        blocks_per_class = L // rows
        delta = (lax.broadcasted_iota(jnp.int32, (rows, wk), 1)
                 - lax.broadcasted_iota(jnp.int32, (rows, wk), 0))

        def tile(t, carry, pi=pi, d=d, radius=radius, L=L, wk=wk,
                 blocks_per_class=blocks_per_class, delta=delta):
            r = t // blocks_per_class
            j = t % blocks_per_class
            q0 = j * rows
            k0 = jnp.clip(q0 - radius, 0, L - wk)
            qb = q_ref[0, pl.ds(q0 * d + r, rows, stride=d), :].astype(BF16)
            kw = k_ref[0, pl.ds(k0 * d + r, wk, stride=d), :].astype(BF16)
            vw = v_ref[0, pl.ds(k0 * d + r, wk, stride=d), :].astype(BF16)
            in_band = jnp.abs(delta + (k0 - q0)) <= radius
            parts = []
            for h in range(2):
                qh = jnp.where(head0 if h == 0 else ~head0, qb, jnp.zeros_like(qb))
                s = lax.dot_general(qh, kw, NT_DIMS, preferred_element_type=F32)
                s = jnp.where(in_band, s, NEG_INF)
                mx = jnp.max(s, axis=-1, keepdims=True)
                p = jnp.exp(s - mx)
                z = jnp.sum(p, axis=-1, keepdims=True)
                n = jnp.dot(p.astype(BF16), vw, preferred_element_type=F32)
                parts.append((n, mx, z))
            sel = lambda a, b: jnp.where(head0, a, b)
            dst = pl.ds(q0 * d + r, rows, stride=d)
            n_scr[pi, dst, :] = sel(parts[0][0], parts[1][0])
            m_scr[pi, dst, :] = sel(parts[0][1], parts[1][1])
            z_scr[pi, dst, :] = sel(parts[0][2], parts[1][2])
            return carry

        lax.fori_loop(0, d * blocks_per_class, tile, 0)

    chunk = 256

    def combine(c, carry):
        sl = pl.ds(pl.multiple_of(c * chunk, chunk), chunk)
        ms = [m_scr[pi, sl, :] for pi in range(len(DIL_PATTERNS))]
        top = functools.reduce(jnp.maximum, ms)
        num = jnp.zeros((chunk, LANES), F32)
        den = jnp.zeros((chunk, LANES), F32)
        for pi in range(len(DIL_PATTERNS)):
            a = jnp.exp(ms[pi] - top)
            num = num + a * n_scr[pi, sl, :]
            den = den + a * z_scr[pi, sl, :]
        o_ref[0, sl, :] = (num / den).astype(o_ref.dtype)
        return carry

    lax.fori_loop(0, S // chunk, combine, 0)


def _dilated_attention(dil_qkv):
    B, S, _ = dil_qkv.shape
    nblk = D_DIL // LANES
    scr = pltpu.VMEM((len(DIL_PATTERNS), S, LANES), F32)
    return pl.pallas_call(
        _dil_kernel,
        grid=(B, nblk),
        in_specs=[
            pl.BlockSpec((1, S, LANES), lambda b, h: (b, 0, h)),
            pl.BlockSpec((1, S, LANES), lambda b, h: (b, 0, nblk + h)),
            pl.BlockSpec((1, S, LANES), lambda b, h: (b, 0, 2 * nblk + h)),
        ],
        out_specs=pl.BlockSpec((1, S, LANES), lambda b, h: (b, 0, h)),
        out_shape=jax.ShapeDtypeStruct((B, S, D_DIL), BF16),
        scratch_shapes=[scr, scr, scr],
        compiler_params=pltpu.CompilerParams(
            dimension_semantics=("parallel", "parallel"),
            vmem_limit_bytes=VMEM_LIMIT_BYTES),
        name="dilated_attn",
    )(dil_qkv, dil_qkv, dil_qkv)


def _post_kernel(x_ref, od_ref, og_ref, p_ref, wo_ref, gm_ref, wu_ref, wd_ref,
                 gp_ref, wg_ref, wp_ref, gf_ref, o_ref):
    mixed = jnp.concatenate([od_ref[...], og_ref[...]], axis=-1)
    h = x_ref[...] + jnp.dot(mixed, wo_ref[...], preferred_element_type=F32)

    hn = _rms(h, gm_ref[...], NORM_EPS).astype(BF16)
    for c in range(D_FF // FF_CHUNK):
        u = jnp.dot(hn, wu_ref[:, c * FF_CHUNK:(c + 1) * FF_CHUNK], preferred_element_type=F32)
        u = jnp.square(jnp.maximum(u, 0.0)).astype(BF16)
        h = h + jnp.dot(u, wd_ref[c * FF_CHUNK:(c + 1) * FF_CHUNK, :], preferred_element_type=F32)

    hn = _rms(h, gp_ref[...], NORM_EPS).astype(BF16)
    gate = jax.nn.sigmoid(jnp.dot(hn, wg_ref[...], preferred_element_type=F32))
    emb = jnp.dot(p_ref[...].astype(BF16), wp_ref[...], preferred_element_type=F32)
    h = h + gate * emb
    o_ref[...] = _rms(h, gf_ref[...], NORM_EPS)


def _post(x2, od2, og2, p2, w_out, g_mlp, w_up, w_down, g_ple, w_gate, w_ple, g_final):
    N, D = x2.shape
    rows = POST_ROWS
    tok = lambda width: pl.BlockSpec((rows, width), lambda i: (i, 0))
    whole = lambda a: pl.BlockSpec(a.shape, lambda i: (0, 0), pipeline_mode=pl.Buffered(1))
    return pl.pallas_call(
        _post_kernel,
        grid=(N // rows,),
        in_specs=[
            tok(D), tok(D_DIFF), tok(D_DIL), tok(D_PLE),
            whole(w_out), whole(g_mlp), whole(w_up), whole(w_down),
            whole(g_ple), whole(w_gate), whole(w_ple), whole(g_final),
        ],
        out_specs=tok(D),
        out_shape=jax.ShapeDtypeStruct((N, D), F32),
        compiler_params=pltpu.CompilerParams(
            dimension_semantics=("parallel",),
            vmem_limit_bytes=VMEM_LIMIT_BYTES),
        name="out_mlp_ple",
    )(x2, od2, og2, p2, w_out, g_mlp, w_up, w_down, g_ple, w_gate, w_ple, g_final)


def _rope_tables(seq):
    inv = ROPE_THETA ** (-jnp.arange(0, HEAD_DIM, 2, dtype=F32) / HEAD_DIM)
    ang = jnp.arange(seq, dtype=F32)[:, None] * inv[None, :]
    cos, sin = jnp.cos(ang), jnp.sin(ang)
    reps = LANES // HEAD_DIM
    cos_t = jnp.tile(jnp.concatenate([cos, cos], axis=-1), (1, reps))
    sin_t = jnp.tile(jnp.concatenate([-sin, sin], axis=-1), (1, reps))
    return cos_t, sin_t


def kernel(x, p, w_in, w_out, g_mix, lambda_q1, lambda_k1, lambda_q2, lambda_k2, g_subln,
           g_mlp, w_up, w_down, g_ple, w_ple_gate, w_ple_proj, g_final):
    B, S, D = x.shape
    assert x.shape[1:] == (2048, D_MODEL) and w_in.shape == (1, D_MODEL, 2 * D_GROUP)
    cos_t, sin_t = _rope_tables(S)

    diff_qkv, dil_qkv = _project(x, g_mix, w_in[0].astype(BF16), cos_t, sin_t)
    od = _diff_attention(diff_qkv, lambda_q1, lambda_k1, lambda_q2, lambda_k2, g_subln)
    og = _dilated_attention(dil_qkv)

    out = _post(
        x.reshape(B * S, D), od.reshape(B * S, D_DIFF), og.reshape(B * S, D_DIL),
        p[0].reshape(B * S, D_PLE),
        w_out[0].astype(BF16), g_mlp, w_up[0].astype(BF16), w_down[0].astype(BF16),
        g_ple, w_ple_gate[0].astype(BF16), w_ple_proj[0].astype(BF16), g_final.reshape(1, D))
    return out.reshape(B, S, D)
```

```python
import functools
import math

import jax
import jax.numpy as jnp
import numpy as np
from jax import lax
from jax.experimental import pallas as pl
from jax.experimental.pallas import tpu as pltpu

D_MODEL = 1024
HEAD_DIM = 64
D_DIFF = 512
D_DIL = 512
D_GROUP = 3 * D_DIFF
N_HEADS_DIFF = 4
DIL_PATTERNS = ((128, 1), (512, 4), (2048, 16))
D_FF = 4 * D_MODEL
D_PLE = 256
ROPE_THETA = 10000.0
NORM_EPS = 1e-6
SUBLN_EPS = 1e-5
NEG_INF = -1e30
LAM_INIT = 0.8 - 0.6 * math.exp(-0.3 * 0)

LANES = 128
VMEM_LIMIT_BYTES = 56 * 1024 * 1024

PROJ_ROWS = 512
DIFF_Q_ROWS = 256
DIL_Q_ROWS = 128
POST_ROWS = 512
FF_CHUNK = 1024

BF16 = jnp.bfloat16
F32 = jnp.float32
NT_DIMS = (((1,), (1,)), ((), ()))


def _rms(x, g, eps):
    return x * lax.rsqrt(jnp.mean(x * x, axis=-1, keepdims=True) + eps) * g


def _proj_kernel(x_ref, g_ref, w_ref, cos_ref, sin_ref, diff_ref, dil_ref):
    x = x_ref[0]
    hn = _rms(x, g_ref[...], NORM_EPS).astype(BF16)
    cos = cos_ref[...]
    sin = sin_ref[...]
    lane = lax.broadcasted_iota(jnp.int32, cos.shape, 1)
    first_half = (lane % HEAD_DIM) < (HEAD_DIM // 2)

    def rope(t):
        swapped = jnp.where(first_half,
                            pltpu.roll(t, LANES - HEAD_DIM // 2, 1),
                            pltpu.roll(t, HEAD_DIM // 2, 1))
        return t * cos + swapped * sin

    scale = HEAD_DIM ** -0.5
    for grp, out_ref in ((0, diff_ref), (1, dil_ref)):
        for part in range(3):
            col0 = grp * D_GROUP + part * D_DIFF
            acc = jnp.dot(hn, w_ref[:, col0:col0 + D_DIFF], preferred_element_type=F32)
            for c in range(D_DIFF // LANES):
                t = acc[:, c * LANES:(c + 1) * LANES]
                if part < 2:
                    t = rope(t)
                if part == 0:
                    t = t * scale
                o0 = part * D_DIFF + c * LANES
                out_ref[0, :, o0:o0 + LANES] = t.astype(out_ref.dtype)


def _project(x, g_mix, w_in, cos_t, sin_t):
    B, S, D = x.shape
    rows = PROJ_ROWS
    return pl.pallas_call(
        _proj_kernel,
        grid=(B, S // rows),
        in_specs=[
            pl.BlockSpec((1, rows, D), lambda b, i: (b, i, 0)),
            pl.BlockSpec((1, D), lambda b, i: (0, 0)),
            pl.BlockSpec((D, 2 * D_GROUP), lambda b, i: (0, 0), pipeline_mode=pl.Buffered(1)),
            pl.BlockSpec((rows, LANES), lambda b, i: (i, 0)),
            pl.BlockSpec((rows, LANES), lambda b, i: (i, 0)),
        ],
        out_specs=[
            pl.BlockSpec((1, rows, D_GROUP), lambda b, i: (b, i, 0)),
            pl.BlockSpec((1, rows, D_GROUP), lambda b, i: (b, i, 0)),
        ],
        out_shape=[
            jax.ShapeDtypeStruct((B, S, D_GROUP), BF16),
            jax.ShapeDtypeStruct((B, S, D_GROUP), F32),
        ],
        compiler_params=pltpu.CompilerParams(
            dimension_semantics=("parallel", "parallel"),
            vmem_limit_bytes=VMEM_LIMIT_BYTES),
        name="proj_rope",
    )(x, g_mix, w_in, cos_t, sin_t)


def _diff_kernel(q_ref, k_ref, v_ref, lq1_ref, lk1_ref, lq2_ref, lk2_ref, g_ref, o_ref):
    q = q_ref[0]
    k = k_ref[0]
    v = v_ref[0]
    lane = lax.broadcasted_iota(jnp.int32, q.shape, 1)
    outs = []
    for m in range(2):
        in_map = (lane >= m * HEAD_DIM) & (lane < (m + 1) * HEAD_DIM)
        qm = jnp.where(in_map, q, jnp.zeros_like(q))
        s = lax.dot_general(qm, k, NT_DIMS, preferred_element_type=F32)
        mx = jnp.max(s, axis=-1, keepdims=True)
        p = jnp.exp(s - mx)
        z = jnp.sum(p, axis=-1, keepdims=True)
        o = jnp.dot(p.astype(BF16), v, preferred_element_type=F32)
        outs.append(o / z)
    lam = (jnp.exp(jnp.sum(lq1_ref[...] * lk1_ref[...], axis=-1, keepdims=True))
           - jnp.exp(jnp.sum(lq2_ref[...] * lk2_ref[...], axis=-1, keepdims=True))
           + LAM_INIT)
    o = outs[0] - lam * outs[1]
    o = _rms(o, g_ref[...], SUBLN_EPS) * (1.0 - LAM_INIT)
    o_ref[0] = o.astype(o_ref.dtype)


def _diff_attention(diff_qkv, lq1, lk1, lq2, lk2, g_subln):
    B, S, _ = diff_qkv.shape
    rows = DIFF_Q_ROWS
    nblk = D_DIFF // LANES
    small = lambda n: pl.BlockSpec((1, n), lambda b, h, i: (0, 0))
    return pl.pallas_call(
        _diff_kernel,
        grid=(B, N_HEADS_DIFF, S // rows),
        in_specs=[
            pl.BlockSpec((1, rows, LANES), lambda b, h, i: (b, i, h)),
            pl.BlockSpec((1, S, LANES), lambda b, h, i: (b, 0, nblk + h)),
            pl.BlockSpec((1, S, LANES), lambda b, h, i: (b, 0, 2 * nblk + h)),
            small(HEAD_DIM), small(HEAD_DIM), small(HEAD_DIM), small(HEAD_DIM),
            small(2 * HEAD_DIM),
        ],
        out_specs=pl.BlockSpec((1, rows, LANES), lambda b, h, i: (b, i, h)),
        out_shape=jax.ShapeDtypeStruct((B, S, D_DIFF), BF16),
        compiler_params=pltpu.CompilerParams(
            dimension_semantics=("parallel", "parallel", "arbitrary"),
            vmem_limit_bytes=VMEM_LIMIT_BYTES),
        name="diff_attn",
    )(diff_qkv, diff_qkv, diff_qkv, lq1, lk1, lq2, lk2, g_subln)


def _dil_tiles(window, d, seq):
    radius, L = window // (2 * d), seq // d
    wk = min(DIL_Q_ROWS + 2 * radius, L)
    tiles = []
    for r in range(d):
        for j in range(L // DIL_Q_ROWS):
            q0 = j * DIL_Q_ROWS
            tiles.append((q0 * d + r, r, q0, min(max(q0 - radius, 0), L - wk)))
    return radius, L, wk, tiles


def _dil_bias(seq):
    out = []
    for window, d in DIL_PATTERNS:
        radius, _, wk, tiles = _dil_tiles(window, d, seq)
        row = np.arange(DIL_Q_ROWS)[:, None]
        col = np.arange(wk)[None, :]
        bias = [np.where(np.abs((k0 + col) - (q0 + row)) <= radius, 0.0, NEG_INF)
                for _, _, q0, k0 in tiles]
        out.append(jnp.asarray(np.stack(bias), dtype=F32))
    return out


def _dil_kernel(q_ref, k_ref, v_ref, b0_ref, b1_ref, b2_ref, o_ref, n_scr, m_scr, z_scr):
    S = q_ref.shape[1]
    rows = DIL_Q_ROWS
    head0 = lax.broadcasted_iota(jnp.int32, (1, 1, LANES), 2) < HEAD_DIM

    for pi, ((window, d), bias_ref) in enumerate(zip(DIL_PATTERNS, (b0_ref, b1_ref, b2_ref))):
        _, L, wk, tiles = _dil_tiles(window, d, S)
        cls = [pl.ds(r, L, stride=d) for r in range(d)]
        qc = [q_ref[0, c, :].astype(BF16) for c in cls]
        kc = [k_ref[0, c, :].astype(BF16) for c in cls]
        vc = [v_ref[0, c, :].astype(BF16) for c in cls]
        qt = jnp.stack([qc[r][q0:q0 + rows] for _, r, q0, _ in tiles])
        kt = jnp.stack([kc[r][k0:k0 + wk] for _, r, _, k0 in tiles])
        vt = jnp.stack([vc[r][k0:k0 + wk] for _, r, _, k0 in tiles])
        bias = bias_ref[...]
        parts = []
        for h in range(2):
            qh = jnp.where(head0 if h == 0 else ~head0, qt, jnp.zeros_like(qt))
            s = jnp.einsum("tqc,tkc->tqk", qh, kt, preferred_element_type=F32) + bias
            mx = jnp.max(s, axis=-1, keepdims=True)
            p = jnp.exp(s - mx)
            z = jnp.sum(p, axis=-1, keepdims=True)
            n = jnp.einsum("tqk,tkc->tqc", p.astype(BF16), vt, preferred_element_type=F32)
            parts.append((n, mx, z))
        n, mx, z = (jnp.where(head0, a, b) for a, b in zip(*parts))
        for t, (start, _, _, _) in enumerate(tiles):
            dst = pl.ds(start, rows, stride=d)
            n_scr[pi, dst, :] = n[t]
            m_scr[pi, dst, :] = mx[t]
            z_scr[pi, dst, :] = z[t]

    chunk = 256

    def combine(c, carry):
        sl = pl.ds(pl.multiple_of(c * chunk, chunk), chunk)
        ms = [m_scr[pi, sl, :] for pi in range(len(DIL_PATTERNS))]
        top = functools.reduce(jnp.maximum, ms)
        num = jnp.zeros((chunk, LANES), F32)
        den = jnp.zeros((chunk, LANES), F32)
        for pi in range(len(DIL_PATTERNS)):
            a = jnp.exp(ms[pi] - top)
            num = num + a * n_scr[pi, sl, :]
            den = den + a * z_scr[pi, sl, :]
        o_ref[0, sl, :] = (num / den).astype(o_ref.dtype)
        return carry

    lax.fori_loop(0, S // chunk, combine, 0)


def _dilated_attention(dil_qkv):
    B, S, _ = dil_qkv.shape
    nblk = D_DIL // LANES
    scr = pltpu.VMEM((len(DIL_PATTERNS), S, LANES), F32)
    biases = _dil_bias(S)
    const = lambda a: pl.BlockSpec(a.shape, lambda b, h: (0, 0, 0), pipeline_mode=pl.Buffered(1))
    return pl.pallas_call(
        _dil_kernel,
        grid=(B, nblk),
        in_specs=[
            pl.BlockSpec((1, S, LANES), lambda b, h: (b, 0, h)),
            pl.BlockSpec((1, S, LANES), lambda b, h: (b, 0, nblk + h)),
            pl.BlockSpec((1, S, LANES), lambda b, h: (b, 0, 2 * nblk + h)),
        ] + [const(a) for a in biases],
        out_specs=pl.BlockSpec((1, S, LANES), lambda b, h: (b, 0, h)),
        out_shape=jax.ShapeDtypeStruct((B, S, D_DIL), BF16),
        scratch_shapes=[scr, scr, scr],
        compiler_params=pltpu.CompilerParams(
            dimension_semantics=("parallel", "parallel"),
            vmem_limit_bytes=VMEM_LIMIT_BYTES),
        name="dilated_attn",
    )(dil_qkv, dil_qkv, dil_qkv, *biases)


def _post_kernel(x_ref, od_ref, og_ref, p_ref, wo_ref, gm_ref, wu_ref, wd_ref,
                 gp_ref, wg_ref, wp_ref, gf_ref, o_ref):
    mixed = jnp.concatenate([od_ref[...], og_ref[...]], axis=-1)
    h = x_ref[...] + jnp.dot(mixed, wo_ref[...], preferred_element_type=F32)

    hn = _rms(h, gm_ref[...], NORM_EPS).astype(BF16)
    for c in range(D_FF // FF_CHUNK):
        u = jnp.dot(hn, wu_ref[:, c * FF_CHUNK:(c + 1) * FF_CHUNK], preferred_element_type=F32)
        u = jnp.square(jnp.maximum(u, 0.0)).astype(BF16)
        h = h + jnp.dot(u, wd_ref[c * FF_CHUNK:(c + 1) * FF_CHUNK, :], preferred_element_type=F32)

    hn = _rms(h, gp_ref[...], NORM_EPS).astype(BF16)
    gate = jax.nn.sigmoid(jnp.dot(hn, wg_ref[...], preferred_element_type=F32))
    emb = jnp.dot(p_ref[...].astype(BF16), wp_ref[...], preferred_element_type=F32)
    h = h + gate * emb
    o_ref[...] = _rms(h, gf_ref[...], NORM_EPS)


def _post(x2, od2, og2, p2, w_out, g_mlp, w_up, w_down, g_ple, w_gate, w_ple, g_final):
    N, D = x2.shape
    rows = POST_ROWS
    tok = lambda width: pl.BlockSpec((rows, width), lambda i: (i, 0))
    whole = lambda a: pl.BlockSpec(a.shape, lambda i: (0, 0), pipeline_mode=pl.Buffered(1))
    return pl.pallas_call(
        _post_kernel,
        grid=(N // rows,),
        in_specs=[
            tok(D), tok(D_DIFF), tok(D_DIL), tok(D_PLE),
            whole(w_out), whole(g_mlp), whole(w_up), whole(w_down),
            whole(g_ple), whole(w_gate), whole(w_ple), whole(g_final),
        ],
        out_specs=tok(D),
        out_shape=jax.ShapeDtypeStruct((N, D), F32),
        compiler_params=pltpu.CompilerParams(
            dimension_semantics=("parallel",),
            vmem_limit_bytes=VMEM_LIMIT_BYTES),
        name="out_mlp_ple",
    )(x2, od2, og2, p2, w_out, g_mlp, w_up, w_down, g_ple, w_gate, w_ple, g_final)


def _rope_tables(seq):
    inv = ROPE_THETA ** (-jnp.arange(0, HEAD_DIM, 2, dtype=F32) / HEAD_DIM)
    ang = jnp.arange(seq, dtype=F32)[:, None] * inv[None, :]
    cos, sin = jnp.cos(ang), jnp.sin(ang)
    reps = LANES // HEAD_DIM
    cos_t = jnp.tile(jnp.concatenate([cos, cos], axis=-1), (1, reps))
    sin_t = jnp.tile(jnp.concatenate([-sin, sin], axis=-1), (1, reps))
    return cos_t, sin_t


def kernel(x, p, w_in, w_out, g_mix, lambda_q1, lambda_k1, lambda_q2, lambda_k2, g_subln,
           g_mlp, w_up, w_down, g_ple, w_ple_gate, w_ple_proj, g_final):
    B, S, D = x.shape
    assert x.shape[1:] == (2048, D_MODEL) and w_in.shape == (1, D_MODEL, 2 * D_GROUP)
    cos_t, sin_t = _rope_tables(S)

    diff_qkv, dil_qkv = _project(x, g_mix, w_in[0].astype(BF16), cos_t, sin_t)
    od = _diff_attention(diff_qkv, lambda_q1, lambda_k1, lambda_q2, lambda_k2, g_subln)
    og = _dilated_attention(dil_qkv)

    out = _post(
        x.reshape(B * S, D), od.reshape(B * S, D_DIFF), og.reshape(B * S, D_DIL),
        p[0].reshape(B * S, D_PLE),
        w_out[0].astype(BF16), g_mlp, w_up[0].astype(BF16), w_down[0].astype(BF16),
        g_ple, w_ple_gate[0].astype(BF16), w_ple_proj[0].astype(BF16), g_final.reshape(1, D))
    return out.reshape(B, S, D)
```

```python
import functools
import math

import jax
import jax.numpy as jnp
import numpy as np
from jax import lax
from jax.experimental import pallas as pl
from jax.experimental.pallas import tpu as pltpu

D_MODEL = 1024
HEAD_DIM = 64
D_DIFF = 512
D_DIL = 512
D_GROUP = 3 * D_DIFF
N_HEADS_DIFF = 4
DIL_PATTERNS = ((128, 1), (512, 4), (2048, 16))
D_FF = 4 * D_MODEL
D_PLE = 256
ROPE_THETA = 10000.0
NORM_EPS = 1e-6
SUBLN_EPS = 1e-5
NEG_INF = -1e30
LAM_INIT = 0.8 - 0.6 * math.exp(-0.3 * 0)

LANES = 128
VMEM_LIMIT_BYTES = 56 * 1024 * 1024

PROJ_ROWS = 512
DIFF_Q_ROWS = 512
DIL_Q_ROWS = 128
POST_ROWS = 512
FF_CHUNK = 1024

BF16 = jnp.bfloat16
F32 = jnp.float32
NT_DIMS = (((1,), (1,)), ((), ()))


def _rms(x, g, eps):
    return x * lax.rsqrt(jnp.mean(x * x, axis=-1, keepdims=True) + eps) * g


def _proj_kernel(x_ref, g_ref, wqt_ref, wvt_ref, w_ref, cos_ref, sin_ref, cost_ref, sint_ref,
                 qt_ref, kd_ref, vt_ref, dil_ref):
    x = x_ref[0]
    hn = _rms(x, g_ref[...], NORM_EPS).astype(BF16)
    cos = cos_ref[...]
    sin = sin_ref[...]
    lane = lax.broadcasted_iota(jnp.int32, cos.shape, 1)
    first_half = (lane % HEAD_DIM) < (HEAD_DIM // 2)
    scale = HEAD_DIM ** -0.5
    half = HEAD_DIM // 2

    def rope(t):
        swapped = jnp.where(first_half,
                            pltpu.roll(t, LANES - half, 1),
                            pltpu.roll(t, half, 1))
        return t * cos + swapped * sin

    qt = lax.dot_general(wqt_ref[...], hn, NT_DIMS, preferred_element_type=F32)
    cos_t = cost_ref[...] * scale
    sin_t = sint_ref[...] * scale
    for hb in range(D_DIFF // HEAD_DIM):
        t1 = qt[hb * HEAD_DIM:hb * HEAD_DIM + half]
        t2 = qt[hb * HEAD_DIM + half:(hb + 1) * HEAD_DIM]
        qt_ref[0, hb * HEAD_DIM:hb * HEAD_DIM + half, :] = (t1 * cos_t - t2 * sin_t).astype(BF16)
        qt_ref[0, hb * HEAD_DIM + half:(hb + 1) * HEAD_DIM, :] = (t1 * sin_t + t2 * cos_t).astype(BF16)

    vt_ref[0] = lax.dot_general(wvt_ref[...], hn, NT_DIMS, preferred_element_type=F32).astype(BF16)

    slabs = ((kd_ref, 0, True, False), (dil_ref, 0, True, True),
             (dil_ref, D_DIL, True, False), (dil_ref, 2 * D_DIL, False, False))
    for n, (out_ref, o0, use_rope, use_scale) in enumerate(slabs):
        acc = jnp.dot(hn, w_ref[:, n * D_DIFF:(n + 1) * D_DIFF], preferred_element_type=F32)
        for c in range(D_DIFF // LANES):
            t = acc[:, c * LANES:(c + 1) * LANES]
            if use_rope:
                t = rope(t)
            if use_scale:
                t = t * scale
            out_ref[0, :, o0 + c * LANES:o0 + (c + 1) * LANES] = t.astype(out_ref.dtype)


def _project(x, g_mix, w_in, tables):
    B, S, D = x.shape
    rows = PROJ_ROWS
    cos_t, sin_t, cos_tt, sin_tt = tables
    wqt = w_in[:, :D_DIFF].T.astype(BF16)
    wvt = w_in[:, 2 * D_DIFF:3 * D_DIFF].T.astype(BF16)
    w_rest = jnp.concatenate([w_in[:, D_DIFF:2 * D_DIFF], w_in[:, D_GROUP:]], axis=1).astype(BF16)
    const = lambda a: pl.BlockSpec(a.shape, lambda b, i: (0, 0), pipeline_mode=pl.Buffered(1))
    return pl.pallas_call(
        _proj_kernel,
        grid=(B, S // rows),
        in_specs=[
            pl.BlockSpec((1, rows, D), lambda b, i: (b, i, 0)),
            pl.BlockSpec((1, D), lambda b, i: (0, 0)),
            const(wqt), const(wvt), const(w_rest),
            pl.BlockSpec((rows, LANES), lambda b, i: (i, 0)),
            pl.BlockSpec((rows, LANES), lambda b, i: (i, 0)),
            pl.BlockSpec((HEAD_DIM // 2, rows), lambda b, i: (0, i)),
            pl.BlockSpec((HEAD_DIM // 2, rows), lambda b, i: (0, i)),
        ],
        out_specs=[
            pl.BlockSpec((1, D_DIFF, rows), lambda b, i: (b, 0, i)),
            pl.BlockSpec((1, rows, D_DIFF), lambda b, i: (b, i, 0)),
            pl.BlockSpec((1, D_DIFF, rows), lambda b, i: (b, 0, i)),
            pl.BlockSpec((1, rows, D_GROUP), lambda b, i: (b, i, 0)),
        ],
        out_shape=[
            jax.ShapeDtypeStruct((B, D_DIFF, S), BF16),
            jax.ShapeDtypeStruct((B, S, D_DIFF), BF16),
            jax.ShapeDtypeStruct((B, D_DIFF, S), BF16),
            jax.ShapeDtypeStruct((B, S, D_GROUP), F32),
        ],
        compiler_params=pltpu.CompilerParams(
            dimension_semantics=("parallel", "parallel"),
            vmem_limit_bytes=VMEM_LIMIT_BYTES),
        name="proj_rope",
    )(x, g_mix, wqt, wvt, w_rest, cos_t, sin_t, cos_tt, sin_tt)


def _diff_kernel(qt_ref, k_ref, vt_ref, lq1_ref, lk1_ref, lq2_ref, lk2_ref, g_ref, o_ref):
    S = k_ref.shape[1]
    rows = DIFF_Q_ROWS
    k = k_ref[0]
    vt = vt_ref[0]
    lam = (jnp.exp(jnp.sum(lq1_ref[...] * lk1_ref[...], axis=-1, keepdims=True))
           - jnp.exp(jnp.sum(lq2_ref[...] * lk2_ref[...], axis=-1, keepdims=True))
           + LAM_INIT)
    gain = g_ref[...] * (1.0 - LAM_INIT)
    feat = lax.broadcasted_iota(jnp.int32, (2 * HEAD_DIM, rows), 0)

    def block(i, carry):
        cols = pl.ds(pl.multiple_of(i * rows, rows), rows)
        qt = qt_ref[0, :, cols]
        outs = []
        for m in range(2):
            in_map = (feat >= m * HEAD_DIM) & (feat < (m + 1) * HEAD_DIM)
            qm = jnp.where(in_map, qt, jnp.zeros_like(qt))
            s = jnp.dot(k, qm, preferred_element_type=F32)
            mx = jnp.max(s, axis=0, keepdims=True)
            p = jnp.exp(s - mx)
            z = jnp.sum(p, axis=0, keepdims=True)
            o = jnp.dot(vt, p.astype(BF16), preferred_element_type=F32)
            outs.append(o / z)
        o = outs[0] - lam * outs[1]
        o = o * lax.rsqrt(jnp.mean(o * o, axis=0, keepdims=True) + SUBLN_EPS) * gain
        o_ref[0, cols, :] = o.T.astype(o_ref.dtype)
        return carry

    lax.fori_loop(0, S // rows, block, 0)


def _diff_attention(qt, kd, vt, lq1, lk1, lq2, lk2, g_subln):
    B, S, _ = kd.shape
    small = lambda n: pl.BlockSpec((1, n), lambda b, h: (0, 0))
    return pl.pallas_call(
        _diff_kernel,
        grid=(B, N_HEADS_DIFF),
        in_specs=[
            pl.BlockSpec((1, LANES, S), lambda b, h: (b, h, 0)),
            pl.BlockSpec((1, S, LANES), lambda b, h: (b, 0, h)),
            pl.BlockSpec((1, LANES, S), lambda b, h: (b, h, 0)),
            small(HEAD_DIM), small(HEAD_DIM), small(HEAD_DIM), small(HEAD_DIM),
            pl.BlockSpec((2 * HEAD_DIM, 1), lambda b, h: (0, 0)),
        ],
        out_specs=pl.BlockSpec((1, S, LANES), lambda b, h: (b, 0, h)),
        out_shape=jax.ShapeDtypeStruct((B, S, D_DIFF), BF16),
        compiler_params=pltpu.CompilerParams(
            dimension_semantics=("parallel", "parallel"),
            vmem_limit_bytes=VMEM_LIMIT_BYTES),
        name="diff_attn",
    )(qt, kd, vt, lq1, lk1, lq2, lk2, g_subln.reshape(2 * HEAD_DIM, 1))


def _dil_tiles(window, d, seq):
    radius, L = window // (2 * d), seq // d
    wk = min(DIL_Q_ROWS + 2 * radius, L)
    tiles = []
    for r in range(d):
        for j in range(L // DIL_Q_ROWS):
            q0 = j * DIL_Q_ROWS
            tiles.append((q0 * d + r, r, q0, min(max(q0 - radius, 0), L - wk)))
    return radius, L, wk, tiles


def _dil_bias(seq):
    out = []
    for window, d in DIL_PATTERNS:
        radius, _, wk, tiles = _dil_tiles(window, d, seq)
        row = np.arange(DIL_Q_ROWS)[:, None]
        col = np.arange(wk)[None, :]
        bias = [np.where(np.abs((k0 + col) - (q0 + row)) <= radius, 0.0, NEG_INF)
                for _, _, q0, k0 in tiles]
        out.append(jnp.asarray(np.stack(bias), dtype=F32))
    return out


def _dil_kernel(q_ref, k_ref, v_ref, b0_ref, b1_ref, b2_ref, o_ref, n_scr, m_scr, z_scr):
    S = q_ref.shape[1]
    rows = DIL_Q_ROWS
    head0 = lax.broadcasted_iota(jnp.int32, (1, 1, LANES), 2) < HEAD_DIM

    for pi, ((window, d), bias_ref) in enumerate(zip(DIL_PATTERNS, (b0_ref, b1_ref, b2_ref))):
        _, L, wk, tiles = _dil_tiles(window, d, S)
        cls = [pl.ds(r, L, stride=d) for r in range(d)]
        qc = [q_ref[0, c, :].astype(BF16) for c in cls]
        kc = [k_ref[0, c, :].astype(BF16) for c in cls]
        vc = [v_ref[0, c, :].astype(BF16) for c in cls]
        qt = jnp.stack([qc[r][q0:q0 + rows] for _, r, q0, _ in tiles])
        kt = jnp.stack([kc[r][k0:k0 + wk] for _, r, _, k0 in tiles])
        vt = jnp.stack([vc[r][k0:k0 + wk] for _, r, _, k0 in tiles])
        bias = bias_ref[...]
        parts = []
        for h in range(2):
            qh = jnp.where(head0 if h == 0 else ~head0, qt, jnp.zeros_like(qt))
            s = jnp.einsum("tqc,tkc->tqk", qh, kt, preferred_element_type=F32) + bias
            mx = jnp.max(s, axis=-1, keepdims=True)
            p = jnp.exp(s - mx)
            z = jnp.sum(p, axis=-1, keepdims=True)
            n = jnp.einsum("tqk,tkc->tqc", p.astype(BF16), vt, preferred_element_type=F32)
            parts.append((n, mx, z))
        n, mx, z = (jnp.where(head0, a, b) for a, b in zip(*parts))
        for t, (start, _, _, _) in enumerate(tiles):
            dst = pl.ds(start, rows, stride=d)
            n_scr[pi, dst, :] = n[t]
            m_scr[pi, dst, :] = mx[t]
            z_scr[pi, dst, :] = z[t]

    chunk = 256

    def combine(c, carry):
        sl = pl.ds(pl.multiple_of(c * chunk, chunk), chunk)
        ms = [m_scr[pi, sl, :] for pi in range(len(DIL_PATTERNS))]
        top = functools.reduce(jnp.maximum, ms)
        num = jnp.zeros((chunk, LANES), F32)
        den = jnp.zeros((chunk, LANES), F32)
        for pi in range(len(DIL_PATTERNS)):
            a = jnp.exp(ms[pi] - top)
            num = num + a * n_scr[pi, sl, :]
            den = den + a * z_scr[pi, sl, :]
        o_ref[0, sl, :] = (num / den).astype(o_ref.dtype)
        return carry

    lax.fori_loop(0, S // chunk, combine, 0)


def _dilated_attention(dil_qkv):
    B, S, _ = dil_qkv.shape
    nblk = D_DIL // LANES
    scr = pltpu.VMEM((len(DIL_PATTERNS), S, LANES), F32)
    biases = _dil_bias(S)
    const = lambda a: pl.BlockSpec(a.shape, lambda b, h: (0, 0, 0), pipeline_mode=pl.Buffered(1))
    return pl.pallas_call(
        _dil_kernel,
        grid=(B, nblk),
        in_specs=[
            pl.BlockSpec((1, S, LANES), lambda b, h: (b, 0, h)),
            pl.BlockSpec((1, S, LANES), lambda b, h: (b, 0, nblk + h)),
            pl.BlockSpec((1, S, LANES), lambda b, h: (b, 0, 2 * nblk + h)),
        ] + [const(a) for a in biases],
        out_specs=pl.BlockSpec((1, S, LANES), lambda b, h: (b, 0, h)),
        out_shape=jax.ShapeDtypeStruct((B, S, D_DIL), BF16),
        scratch_shapes=[scr, scr, scr],
        compiler_params=pltpu.CompilerParams(
            dimension_semantics=("parallel", "parallel"),
            vmem_limit_bytes=VMEM_LIMIT_BYTES),
        name="dilated_attn",
    )(dil_qkv, dil_qkv, dil_qkv, *biases)


def _post_kernel(x_ref, od_ref, og_ref, p_ref, wo_ref, gm_ref, wu_ref, wd_ref,
                 gp_ref, wg_ref, wp_ref, gf_ref, o_ref):
    mixed = jnp.concatenate([od_ref[...], og_ref[...]], axis=-1)
    h = x_ref[...] + jnp.dot(mixed, wo_ref[...], preferred_element_type=F32)

    hn = _rms(h, gm_ref[...], NORM_EPS).astype(BF16)
    for c in range(D_FF // FF_CHUNK):
        u = jnp.dot(hn, wu_ref[:, c * FF_CHUNK:(c + 1) * FF_CHUNK], preferred_element_type=F32)
        u = jnp.square(jnp.maximum(u, 0.0)).astype(BF16)
        h = h + jnp.dot(u, wd_ref[c * FF_CHUNK:(c + 1) * FF_CHUNK, :], preferred_element_type=F32)

    hn = _rms(h, gp_ref[...], NORM_EPS).astype(BF16)
    gate = jax.nn.sigmoid(jnp.dot(hn, wg_ref[...], preferred_element_type=F32))
    emb = jnp.dot(p_ref[...].astype(BF16), wp_ref[...], preferred_element_type=F32)
    h = h + gate * emb
    o_ref[...] = _rms(h, gf_ref[...], NORM_EPS)


def _post(x2, od2, og2, p2, w_out, g_mlp, w_up, w_down, g_ple, w_gate, w_ple, g_final):
    N, D = x2.shape
    rows = POST_ROWS
    tok = lambda width: pl.BlockSpec((rows, width), lambda i: (i, 0))
    whole = lambda a: pl.BlockSpec(a.shape, lambda i: (0, 0), pipeline_mode=pl.Buffered(1))
    return pl.pallas_call(
        _post_kernel,
        grid=(N // rows,),
        in_specs=[
            tok(D), tok(D_DIFF), tok(D_DIL), tok(D_PLE),
            whole(w_out), whole(g_mlp), whole(w_up), whole(w_down),
            whole(g_ple), whole(w_gate), whole(w_ple), whole(g_final),
        ],
        out_specs=tok(D),
        out_shape=jax.ShapeDtypeStruct((N, D), F32),
        compiler_params=pltpu.CompilerParams(
            dimension_semantics=("parallel",),
            vmem_limit_bytes=VMEM_LIMIT_BYTES),
        name="out_mlp_ple",
    )(x2, od2, og2, p2, w_out, g_mlp, w_up, w_down, g_ple, w_gate, w_ple, g_final)


def _rope_tables(seq):
    inv = ROPE_THETA ** (-jnp.arange(0, HEAD_DIM, 2, dtype=F32) / HEAD_DIM)
    ang = jnp.arange(seq, dtype=F32)[:, None] * inv[None, :]
    cos, sin = jnp.cos(ang), jnp.sin(ang)
    reps = LANES // HEAD_DIM
    cos_t = jnp.tile(jnp.concatenate([cos, cos], axis=-1), (1, reps))
    sin_t = jnp.tile(jnp.concatenate([-sin, sin], axis=-1), (1, reps))
    return cos_t, sin_t, cos.T, sin.T


def kernel(x, p, w_in, w_out, g_mix, lambda_q1, lambda_k1, lambda_q2, lambda_k2, g_subln,
           g_mlp, w_up, w_down, g_ple, w_ple_gate, w_ple_proj, g_final):
    B, S, D = x.shape
    assert x.shape[1:] == (2048, D_MODEL) and w_in.shape == (1, D_MODEL, 2 * D_GROUP)

    qt, kd, vt, dil_qkv = _project(x, g_mix, w_in[0], _rope_tables(S))
    od = _diff_attention(qt, kd, vt, lambda_q1, lambda_k1, lambda_q2, lambda_k2, g_subln)
    og = _dilated_attention(dil_qkv)

    out = _post(
        x.reshape(B * S, D), od.reshape(B * S, D_DIFF), og.reshape(B * S, D_DIL),
        p[0].reshape(B * S, D_PLE),
        w_out[0].astype(BF16), g_mlp, w_up[0].astype(BF16), w_down[0].astype(BF16),
        g_ple, w_ple_gate[0].astype(BF16), w_ple_proj[0].astype(BF16), g_final.reshape(1, D))
    return out.reshape(B, S, D)
```

```python
import functools
import math

import jax
import jax.numpy as jnp
import numpy as np
from jax import lax
from jax.experimental import pallas as pl
from jax.experimental.pallas import tpu as pltpu

D_MODEL = 1024
HEAD_DIM = 64
D_DIFF = 512
D_DIL = 512
D_GROUP = 3 * D_DIFF
N_HEADS_DIFF = 4
DIL_PATTERNS = ((128, 1), (512, 4), (2048, 16))
D_FF = 4 * D_MODEL
D_PLE = 256
ROPE_THETA = 10000.0
NORM_EPS = 1e-6
SUBLN_EPS = 1e-5
NEG_INF = -1e30
LAM_INIT = 0.8 - 0.6 * math.exp(-0.3 * 0)
LOG2_E = math.log2(math.e)

LANES = 128
VMEM_LIMIT_BYTES = 56 * 1024 * 1024

PROJ_ROWS = 512
DIFF_Q_ROWS = 512
DIL_Q_ROWS = 128
POST_ROWS = 512
FF_CHUNK = 1024

BF16 = jnp.bfloat16
F32 = jnp.float32
NT_DIMS = (((1,), (1,)), ((), ()))


def _rms(x, g, eps):
    return x * lax.rsqrt(jnp.mean(x * x, axis=-1, keepdims=True) + eps) * g


def _proj_kernel(x_ref, g_ref, wqt_ref, wvt_ref, w_ref, cos_ref, sin_ref, cost_ref, sint_ref,
                 qt_ref, kd_ref, vt_ref, dil_ref):
    x = x_ref[0]
    hn = _rms(x, g_ref[...], NORM_EPS).astype(BF16)
    cos = cos_ref[...]
    sin = sin_ref[...]
    lane = lax.broadcasted_iota(jnp.int32, cos.shape, 1)
    first_half = (lane % HEAD_DIM) < (HEAD_DIM // 2)
    scale = HEAD_DIM ** -0.5
    half = HEAD_DIM // 2

    def rope(t):
        swapped = jnp.where(first_half,
                            pltpu.roll(t, LANES - half, 1),
                            pltpu.roll(t, half, 1))
        return t * cos + swapped * sin

    qt = lax.dot_general(wqt_ref[...], hn, NT_DIMS, preferred_element_type=F32)
    cos_t = cost_ref[...] * (scale * LOG2_E)
    sin_t = sint_ref[...] * (scale * LOG2_E)
    for hb in range(D_DIFF // HEAD_DIM):
        t1 = qt[hb * HEAD_DIM:hb * HEAD_DIM + half]
        t2 = qt[hb * HEAD_DIM + half:(hb + 1) * HEAD_DIM]
        qt_ref[0, hb * HEAD_DIM:hb * HEAD_DIM + half, :] = (t1 * cos_t - t2 * sin_t).astype(BF16)
        qt_ref[0, hb * HEAD_DIM + half:(hb + 1) * HEAD_DIM, :] = (t1 * sin_t + t2 * cos_t).astype(BF16)

    vt_ref[0] = lax.dot_general(wvt_ref[...], hn, NT_DIMS, preferred_element_type=F32).astype(BF16)

    slabs = ((kd_ref, 0, True, False), (dil_ref, 0, True, True),
             (dil_ref, D_DIL, True, False), (dil_ref, 2 * D_DIL, False, False))
    for n, (out_ref, o0, use_rope, use_scale) in enumerate(slabs):
        acc = jnp.dot(hn, w_ref[:, n * D_DIFF:(n + 1) * D_DIFF], preferred_element_type=F32)
        for c in range(D_DIFF // LANES):
            t = acc[:, c * LANES:(c + 1) * LANES]
            if use_rope:
                t = rope(t)
            if use_scale:
                t = t * scale
            out_ref[0, :, o0 + c * LANES:o0 + (c + 1) * LANES] = t.astype(out_ref.dtype)


def _project(x, g_mix, w_in, tables):
    B, S, D = x.shape
    rows = PROJ_ROWS
    cos_t, sin_t, cos_tt, sin_tt = tables
    wqt = w_in[:, :D_DIFF].T.astype(BF16)
    wvt = w_in[:, 2 * D_DIFF:3 * D_DIFF].T.astype(BF16)
    w_rest = jnp.concatenate([w_in[:, D_DIFF:2 * D_DIFF], w_in[:, D_GROUP:]], axis=1).astype(BF16)
    const = lambda a: pl.BlockSpec(a.shape, lambda b, i: (0, 0), pipeline_mode=pl.Buffered(1))
    return pl.pallas_call(
        _proj_kernel,
        grid=(B, S // rows),
        in_specs=[
            pl.BlockSpec((1, rows, D), lambda b, i: (b, i, 0)),
            pl.BlockSpec((1, D), lambda b, i: (0, 0)),
            const(wqt), const(wvt), const(w_rest),
            pl.BlockSpec((rows, LANES), lambda b, i: (i, 0)),
            pl.BlockSpec((rows, LANES), lambda b, i: (i, 0)),
            pl.BlockSpec((HEAD_DIM // 2, rows), lambda b, i: (0, i)),
            pl.BlockSpec((HEAD_DIM // 2, rows), lambda b, i: (0, i)),
        ],
        out_specs=[
            pl.BlockSpec((1, D_DIFF, rows), lambda b, i: (b, 0, i)),
            pl.BlockSpec((1, rows, D_DIFF), lambda b, i: (b, i, 0)),
            pl.BlockSpec((1, D_DIFF, rows), lambda b, i: (b, 0, i)),
            pl.BlockSpec((1, rows, D_GROUP), lambda b, i: (b, i, 0)),
        ],
        out_shape=[
            jax.ShapeDtypeStruct((B, D_DIFF, S), BF16),
            jax.ShapeDtypeStruct((B, S, D_DIFF), BF16),
            jax.ShapeDtypeStruct((B, D_DIFF, S), BF16),
            jax.ShapeDtypeStruct((B, S, D_GROUP), F32),
        ],
        compiler_params=pltpu.CompilerParams(
            dimension_semantics=("parallel", "parallel"),
            vmem_limit_bytes=VMEM_LIMIT_BYTES),
        name="proj_rope",
    )(x, g_mix, wqt, wvt, w_rest, cos_t, sin_t, cos_tt, sin_tt)


def _diff_kernel(qt_ref, k_ref, vt_ref, lq1_ref, lk1_ref, lq2_ref, lk2_ref, g_ref, o_ref,
                 s0_scr, s1_scr):
    S = k_ref.shape[1]
    rows = DIFF_Q_ROWS
    k = k_ref[0]
    vt = vt_ref[0]
    lam = (jnp.exp(jnp.sum(lq1_ref[...] * lk1_ref[...], axis=-1, keepdims=True))
           - jnp.exp(jnp.sum(lq2_ref[...] * lk2_ref[...], axis=-1, keepdims=True))
           + LAM_INIT)
    gain = g_ref[...] * (1.0 - LAM_INIT)
    feat = lax.broadcasted_iota(jnp.int32, (2 * HEAD_DIM, rows), 0)
    s_scr = (s0_scr, s1_scr)
    nblocks = S // rows

    def cols(j):
        return pl.ds(pl.multiple_of(j * rows, rows), rows)

    def scores(j, m):
        qt = qt_ref[0, :, cols(j)]
        in_map = (feat >= m * HEAD_DIM) & (feat < (m + 1) * HEAD_DIM)
        qm = jnp.where(in_map, qt, jnp.zeros_like(qt))
        s = jnp.dot(k, qm, preferred_element_type=F32)
        s_scr[m][...] = s
        return jnp.max(s, axis=0, keepdims=True)

    def attend(m, mx):
        p = jnp.exp2(s_scr[m][...] - mx)
        z = jnp.sum(p, axis=0, keepdims=True)
        o = jnp.dot(vt, p.astype(BF16), preferred_element_type=F32)
        return o / z

    def finish(j, o0, o1):
        o = o0 - lam * o1
        o = o * lax.rsqrt(jnp.mean(o * o, axis=0, keepdims=True) + SUBLN_EPS) * gain
        o_ref[0, cols(j), :] = o.T.astype(o_ref.dtype)

    def block(j, mx0):
        mx1 = scores(j, 1)
        o0 = attend(0, mx0)
        mx0_next = scores(j + 1, 0)
        o1 = attend(1, mx1)
        finish(j, o0, o1)
        return mx0_next

    mx0 = lax.fori_loop(0, nblocks - 1, block, scores(0, 0))
    last = nblocks - 1
    mx1 = scores(last, 1)
    o0 = attend(0, mx0)
    finish(last, o0, attend(1, mx1))


def _diff_attention(qt, kd, vt, lq1, lk1, lq2, lk2, g_subln):
    B, S, _ = kd.shape
    small = lambda n: pl.BlockSpec((1, n), lambda b, h: (0, 0))
    s_scr = pltpu.VMEM((S, DIFF_Q_ROWS), F32)
    return pl.pallas_call(
        _diff_kernel,
        grid=(B, N_HEADS_DIFF),
        in_specs=[
            pl.BlockSpec((1, LANES, S), lambda b, h: (b, h, 0)),
            pl.BlockSpec((1, S, LANES), lambda b, h: (b, 0, h)),
            pl.BlockSpec((1, LANES, S), lambda b, h: (b, h, 0)),
            small(HEAD_DIM), small(HEAD_DIM), small(HEAD_DIM), small(HEAD_DIM),
            pl.BlockSpec((2 * HEAD_DIM, 1), lambda b, h: (0, 0)),
        ],
        out_specs=pl.BlockSpec((1, S, LANES), lambda b, h: (b, 0, h)),
        out_shape=jax.ShapeDtypeStruct((B, S, D_DIFF), BF16),
        scratch_shapes=[s_scr, s_scr],
        compiler_params=pltpu.CompilerParams(
            dimension_semantics=("parallel", "parallel"),
            vmem_limit_bytes=VMEM_LIMIT_BYTES),
        name="diff_attn",
    )(qt, kd, vt, lq1, lk1, lq2, lk2, g_subln.reshape(2 * HEAD_DIM, 1))


def _dil_tiles(window, d, seq):
    radius, L = window // (2 * d), seq // d
    wk = min(DIL_Q_ROWS + 2 * radius, L)
    tiles = []
    for r in range(d):
        for j in range(L // DIL_Q_ROWS):
            q0 = j * DIL_Q_ROWS
            tiles.append((q0 * d + r, r, q0, min(max(q0 - radius, 0), L - wk)))
    return radius, L, wk, tiles


def _dil_bias(seq):
    out = []
    for window, d in DIL_PATTERNS:
        radius, _, wk, tiles = _dil_tiles(window, d, seq)
        row = np.arange(DIL_Q_ROWS)[:, None]
        col = np.arange(wk)[None, :]
        bias = [np.where(np.abs((k0 + col) - (q0 + row)) <= radius, 0.0, NEG_INF)
                for _, _, q0, k0 in tiles]
        out.append(jnp.asarray(np.stack(bias), dtype=F32))
    return out


def _dil_kernel(q_ref, k_ref, v_ref, b0_ref, b1_ref, b2_ref, o_ref, n_scr, m_scr, z_scr):
    S = q_ref.shape[1]
    rows = DIL_Q_ROWS
    head0 = lax.broadcasted_iota(jnp.int32, (1, 1, LANES), 2) < HEAD_DIM

    for pi, ((window, d), bias_ref) in enumerate(zip(DIL_PATTERNS, (b0_ref, b1_ref, b2_ref))):
        _, L, wk, tiles = _dil_tiles(window, d, S)
        cls = [pl.ds(r, L, stride=d) for r in range(d)]
        qc = [q_ref[0, c, :].astype(BF16) for c in cls]
        kc = [k_ref[0, c, :].astype(BF16) for c in cls]
        vc = [v_ref[0, c, :].astype(BF16) for c in cls]
        qt = jnp.stack([qc[r][q0:q0 + rows] for _, r, q0, _ in tiles])
        kt = jnp.stack([kc[r][k0:k0 + wk] for _, r, _, k0 in tiles])
        vt = jnp.stack([vc[r][k0:k0 + wk] for _, r, _, k0 in tiles])
        bias = bias_ref[...]
        parts = []
        for h in range(2):
            qh = jnp.where(head0 if h == 0 else ~head0, qt, jnp.zeros_like(qt))
            s = jnp.einsum("tqc,tkc->tqk", qh, kt, preferred_element_type=F32) + bias
            mx = jnp.max(s, axis=-1, keepdims=True)
            p = jnp.exp(s - mx)
            z = jnp.sum(p, axis=-1, keepdims=True)
            n = jnp.einsum("tqk,tkc->tqc", p.astype(BF16), vt, preferred_element_type=F32)
            parts.append((n, mx, z))
        n, mx, z = (jnp.where(head0, a, b) for a, b in zip(*parts))
        for t, (start, _, _, _) in enumerate(tiles):
            dst = pl.ds(start, rows, stride=d)
            n_scr[pi, dst, :] = n[t]
            m_scr[pi, dst, :] = mx[t]
            z_scr[pi, dst, :] = z[t]

    chunk = 256

    def combine(c, carry):
        sl = pl.ds(pl.multiple_of(c * chunk, chunk), chunk)
        ms = [m_scr[pi, sl, :] for pi in range(len(DIL_PATTERNS))]
        top = functools.reduce(jnp.maximum, ms)
        num = jnp.zeros((chunk, LANES), F32)
        den = jnp.zeros((chunk, LANES), F32)
        for pi in range(len(DIL_PATTERNS)):
            a = jnp.exp(ms[pi] - top)
            num = num + a * n_scr[pi, sl, :]
            den = den + a * z_scr[pi, sl, :]
        o_ref[0, sl, :] = (num / den).astype(o_ref.dtype)
        return carry

    lax.fori_loop(0, S // chunk, combine, 0)


def _dilated_attention(dil_qkv):
    B, S, _ = dil_qkv.shape
    nblk = D_DIL // LANES
    scr = pltpu.VMEM((len(DIL_PATTERNS), S, LANES), F32)
    biases = _dil_bias(S)
    const = lambda a: pl.BlockSpec(a.shape, lambda b, h: (0, 0, 0), pipeline_mode=pl.Buffered(1))
    return pl.pallas_call(
        _dil_kernel,
        grid=(B, nblk),
        in_specs=[
            pl.BlockSpec((1, S, LANES), lambda b, h: (b, 0, h)),
            pl.BlockSpec((1, S, LANES), lambda b, h: (b, 0, nblk + h)),
            pl.BlockSpec((1, S, LANES), lambda b, h: (b, 0, 2 * nblk + h)),
        ] + [const(a) for a in biases],
        out_specs=pl.BlockSpec((1, S, LANES), lambda b, h: (b, 0, h)),
        out_shape=jax.ShapeDtypeStruct((B, S, D_DIL), BF16),
        scratch_shapes=[scr, scr, scr],
        compiler_params=pltpu.CompilerParams(
            dimension_semantics=("parallel", "parallel"),
            vmem_limit_bytes=VMEM_LIMIT_BYTES),
        name="dilated_attn",
    )(dil_qkv, dil_qkv, dil_qkv, *biases)


def _post_kernel(x_ref, od_ref, og_ref, p_ref, wo_ref, gm_ref, wu_ref, wd_ref,
                 gp_ref, wg_ref, wp_ref, gf_ref, o_ref):
    mixed = jnp.concatenate([od_ref[...], og_ref[...]], axis=-1)
    h = x_ref[...] + jnp.dot(mixed, wo_ref[...], preferred_element_type=F32)

    hn = _rms(h, gm_ref[...], NORM_EPS).astype(BF16)
    for c in range(D_FF // FF_CHUNK):
        u = jnp.dot(hn, wu_ref[:, c * FF_CHUNK:(c + 1) * FF_CHUNK], preferred_element_type=F32)
        u = jnp.square(jnp.maximum(u, 0.0)).astype(BF16)
        h = h + jnp.dot(u, wd_ref[c * FF_CHUNK:(c + 1) * FF_CHUNK, :], preferred_element_type=F32)

    hn = _rms(h, gp_ref[...], NORM_EPS).astype(BF16)
    gate = jax.nn.sigmoid(jnp.dot(hn, wg_ref[...], preferred_element_type=F32))
    emb = jnp.dot(p_ref[...].astype(BF16), wp_ref[...], preferred_element_type=F32)
    h = h + gate * emb
    o_ref[...] = _rms(h, gf_ref[...], NORM_EPS)


def _post(x2, od2, og2, p2, w_out, g_mlp, w_up, w_down, g_ple, w_gate, w_ple, g_final):
    N, D = x2.shape
    rows = POST_ROWS
    tok = lambda width: pl.BlockSpec((rows, width), lambda i: (i, 0))
    whole = lambda a: pl.BlockSpec(a.shape, lambda i: (0, 0), pipeline_mode=pl.Buffered(1))
    return pl.pallas_call(
        _post_kernel,
        grid=(N // rows,),
        in_specs=[
            tok(D), tok(D_DIFF), tok(D_DIL), tok(D_PLE),
            whole(w_out), whole(g_mlp), whole(w_up), whole(w_down),
            whole(g_ple), whole(w_gate), whole(w_ple), whole(g_final),
        ],
        out_specs=tok(D),
        out_shape=jax.ShapeDtypeStruct((N, D), F32),
        compiler_params=pltpu.CompilerParams(
            dimension_semantics=("parallel",),
            vmem_limit_bytes=VMEM_LIMIT_BYTES),
        name="out_mlp_ple",
    )(x2, od2, og2, p2, w_out, g_mlp, w_up, w_down, g_ple, w_gate, w_ple, g_final)


def _rope_tables(seq):
    inv = ROPE_THETA ** (-jnp.arange(0, HEAD_DIM, 2, dtype=F32) / HEAD_DIM)
    ang = jnp.arange(seq, dtype=F32)[:, None] * inv[None, :]
    cos, sin = jnp.cos(ang), jnp.sin(ang)
    reps = LANES // HEAD_DIM
    cos_t = jnp.tile(jnp.concatenate([cos, cos], axis=-1), (1, reps))
    sin_t = jnp.tile(jnp.concatenate([-sin, sin], axis=-1), (1, reps))
    return cos_t, sin_t, cos.T, sin.T


def kernel(x, p, w_in, w_out, g_mix, lambda_q1, lambda_k1, lambda_q2, lambda_k2, g_subln,
           g_mlp, w_up, w_down, g_ple, w_ple_gate, w_ple_proj, g_final):
    B, S, D = x.shape
    assert x.shape[1:] == (2048, D_MODEL) and w_in.shape == (1, D_MODEL, 2 * D_GROUP)

    qt, kd, vt, dil_qkv = _project(x, g_mix, w_in[0], _rope_tables(S))
    od = _diff_attention(qt, kd, vt, lambda_q1, lambda_k1, lambda_q2, lambda_k2, g_subln)
    og = _dilated_attention(dil_qkv)

    out = _post(
        x.reshape(B * S, D), od.reshape(B * S, D_DIFF), og.reshape(B * S, D_DIL),
        p[0].reshape(B * S, D_PLE),
        w_out[0].astype(BF16), g_mlp, w_up[0].astype(BF16), w_down[0].astype(BF16),
        g_ple, w_ple_gate[0].astype(BF16), w_ple_proj[0].astype(BF16), g_final.reshape(1, D))
    return out.reshape(B, S, D)
```

```python
import functools
import math

import jax
import jax.numpy as jnp
import numpy as np
from jax import lax
from jax.experimental import pallas as pl
from jax.experimental.pallas import tpu as pltpu

D_MODEL = 1024
HEAD_DIM = 64
D_DIFF = 512
D_DIL = 512
D_GROUP = 3 * D_DIFF
N_HEADS_DIFF = 4
DIL_PATTERNS = ((128, 1), (512, 4), (2048, 16))
D_FF = 4 * D_MODEL
D_PLE = 256
ROPE_THETA = 10000.0
NORM_EPS = 1e-6
SUBLN_EPS = 1e-5
NEG_INF = -1e30
LAM_INIT = 0.8 - 0.6 * math.exp(-0.3 * 0)
LOG2_E = math.log2(math.e)

LANES = 128
BF16_SUBLANES = 16
VMEM_LIMIT_BYTES = 56 * 1024 * 1024

PROJ_ROWS = 512
DIFF_Q_ROWS = 512
DIL_Q_ROWS = 128
POST_ROWS = 512
FF_CHUNK = 1024

BF16 = jnp.bfloat16
F32 = jnp.float32
NT_DIMS = (((1,), (1,)), ((), ()))


def _rms(x, g, eps):
    return x * lax.rsqrt(jnp.mean(x * x, axis=-1, keepdims=True) + eps) * g


def _proj_kernel(x_ref, g_ref, wqt_ref, wvt_ref, w_ref, cos_ref, sin_ref, cost_ref, sint_ref,
                 qt_ref, kd_ref, vt_ref, dil_ref):
    x = x_ref[0]
    hn = _rms(x, g_ref[...], NORM_EPS).astype(BF16)
    cos = cos_ref[...]
    sin = sin_ref[...]
    lane = lax.broadcasted_iota(jnp.int32, cos.shape, 1)
    first_half = (lane % HEAD_DIM) < (HEAD_DIM // 2)
    scale = HEAD_DIM ** -0.5
    half = HEAD_DIM // 2

    def rope(t):
        swapped = jnp.where(first_half,
                            pltpu.roll(t, LANES - half, 1),
                            pltpu.roll(t, half, 1))
        return t * cos + swapped * sin

    qt = lax.dot_general(wqt_ref[...], hn, NT_DIMS, preferred_element_type=F32)
    cos_t = cost_ref[...] * (scale * LOG2_E)
    sin_t = sint_ref[...] * (scale * LOG2_E)
    for hb in range(D_DIFF // HEAD_DIM):
        t1 = qt[hb * HEAD_DIM:hb * HEAD_DIM + half]
        t2 = qt[hb * HEAD_DIM + half:(hb + 1) * HEAD_DIM]
        qt_ref[0, hb * HEAD_DIM:hb * HEAD_DIM + half, :] = (t1 * cos_t - t2 * sin_t).astype(BF16)
        qt_ref[0, hb * HEAD_DIM + half:(hb + 1) * HEAD_DIM, :] = (t1 * sin_t + t2 * cos_t).astype(BF16)

    vt_ref[0] = lax.dot_general(wvt_ref[...], hn, NT_DIMS, preferred_element_type=F32).astype(BF16)

    slabs = ((kd_ref, 0, True, False), (dil_ref, 0, True, True),
             (dil_ref, D_DIL, True, False), (dil_ref, 2 * D_DIL, False, False))
    for n, (out_ref, o0, use_rope, use_scale) in enumerate(slabs):
        acc = jnp.dot(hn, w_ref[:, n * D_DIFF:(n + 1) * D_DIFF], preferred_element_type=F32)
        for c in range(D_DIFF // LANES):
            t = acc[:, c * LANES:(c + 1) * LANES]
            if use_rope:
                t = rope(t)
            if use_scale:
                t = t * scale
            out_ref[0, :, o0 + c * LANES:o0 + (c + 1) * LANES] = t.astype(out_ref.dtype)


def _project(x, g_mix, w_in, tables):
    B, S, D = x.shape
    rows = PROJ_ROWS
    cos_t, sin_t, cos_tt, sin_tt = tables
    wqt = w_in[:, :D_DIFF].T.astype(BF16)
    wvt = w_in[:, 2 * D_DIFF:3 * D_DIFF].T.astype(BF16)
    w_rest = jnp.concatenate([w_in[:, D_DIFF:2 * D_DIFF], w_in[:, D_GROUP:]], axis=1).astype(BF16)
    const = lambda a: pl.BlockSpec(a.shape, lambda b, i: (0, 0), pipeline_mode=pl.Buffered(1))
    return pl.pallas_call(
        _proj_kernel,
        grid=(B, S // rows),
        in_specs=[
            pl.BlockSpec((1, rows, D), lambda b, i: (b, i, 0)),
            pl.BlockSpec((1, D), lambda b, i: (0, 0)),
            const(wqt), const(wvt), const(w_rest),
            pl.BlockSpec((rows, LANES), lambda b, i: (i, 0)),
            pl.BlockSpec((rows, LANES), lambda b, i: (i, 0)),
            pl.BlockSpec((HEAD_DIM // 2, rows), lambda b, i: (0, i)),
            pl.BlockSpec((HEAD_DIM // 2, rows), lambda b, i: (0, i)),
        ],
        out_specs=[
            pl.BlockSpec((1, D_DIFF, rows), lambda b, i: (b, 0, i)),
            pl.BlockSpec((1, rows, D_DIFF), lambda b, i: (b, i, 0)),
            pl.BlockSpec((1, D_DIFF, rows), lambda b, i: (b, 0, i)),
            pl.BlockSpec((1, rows, D_GROUP), lambda b, i: (b, i, 0)),
        ],
        out_shape=[
            jax.ShapeDtypeStruct((B, D_DIFF, S), BF16),
            jax.ShapeDtypeStruct((B, S, D_DIFF), BF16),
            jax.ShapeDtypeStruct((B, D_DIFF, S), BF16),
            jax.ShapeDtypeStruct((B, S, D_GROUP), F32),
        ],
        compiler_params=pltpu.CompilerParams(
            dimension_semantics=("parallel", "parallel"),
            vmem_limit_bytes=VMEM_LIMIT_BYTES),
        name="proj_rope",
    )(x, g_mix, wqt, wvt, w_rest, cos_t, sin_t, cos_tt, sin_tt)


def _dil_geometry(window, d, seq):
    radius, L = window // (2 * d), seq // d
    wk = min(DIL_Q_ROWS + 2 * radius, L)
    return radius, L, wk, L // DIL_Q_ROWS


def _dil_bias(seq):
    out = []
    for window, d in DIL_PATTERNS:
        radius, L, wk, blocks = _dil_geometry(window, d, seq)
        row = np.arange(DIL_Q_ROWS)[:, None]
        col = np.arange(wk)[None, :]
        bias = []
        for t in range(d * blocks):
            q0 = (t % blocks) * DIL_Q_ROWS
            k0 = min(max(q0 - radius, 0), L - wk)
            bias.append(np.where(np.abs((k0 + col) - (q0 + row)) <= radius, 0.0, NEG_INF))
        out.append(jnp.asarray(np.stack(bias), dtype=F32))
    return out


def _attn_kernel(qt_ref, k_ref, vt_ref, lq1_ref, lk1_ref, lq2_ref, lk2_ref, g_ref,
                 dq_ref, dk_ref, dv_ref, b0_ref, b1_ref, b2_ref,
                 od_ref, og_ref, s0_scr, s1_scr, n_scr, m_scr, z_scr):
    S = k_ref.shape[1]
    rows = DIFF_Q_ROWS
    nblocks = S // rows
    k = k_ref[0]
    vt_ones = jnp.concatenate([vt_ref[0], jnp.ones((BF16_SUBLANES, S), BF16)], axis=0)
    lam = (jnp.exp(jnp.sum(lq1_ref[...] * lk1_ref[...], axis=-1, keepdims=True))
           - jnp.exp(jnp.sum(lq2_ref[...] * lk2_ref[...], axis=-1, keepdims=True))
           + LAM_INIT)
    gain = g_ref[...] * (1.0 - LAM_INIT)
    feat = lax.broadcasted_iota(jnp.int32, (2 * HEAD_DIM, rows), 0)
    s_scr = (s0_scr, s1_scr)

    def cols(j):
        return pl.ds(j * rows, rows)

    def scores(j, m):
        qt = qt_ref[0, :, cols(j)]
        in_map = (feat >= m * HEAD_DIM) & (feat < (m + 1) * HEAD_DIM)
        qm = jnp.where(in_map, qt, jnp.zeros_like(qt))
        s = jnp.dot(k, qm, preferred_element_type=F32)
        s_scr[m][...] = s
        return jnp.max(s, axis=0, keepdims=True)

    def attend(m, mx):
        p = jnp.exp2(s_scr[m][...] - mx).astype(BF16)
        o = jnp.dot(vt_ones, p, preferred_element_type=F32)
        return o[:2 * HEAD_DIM] / o[2 * HEAD_DIM:2 * HEAD_DIM + 1]

    def finish(j, o0, o1):
        o = o0 - lam * o1
        o = o * lax.rsqrt(jnp.mean(o * o, axis=0, keepdims=True) + SUBLN_EPS) * gain
        od_ref[0, cols(j), :] = o.T.astype(od_ref.dtype)

    qrows = DIL_Q_ROWS
    head0 = lax.broadcasted_iota(jnp.int32, (1, 1, LANES), 2) < HEAD_DIM

    def dilated_scores(j, pi):
        (window, d), bias_ref = DIL_PATTERNS[pi], (b0_ref, b1_ref, b2_ref)[pi]
        radius, L, wk, blocks = _dil_geometry(window, d, S)
        per = d * blocks // nblocks
        qsl, ksl = [], []
        for i in range(per):
            t = j * per + i
            r, q0 = t // blocks, (t % blocks) * qrows
            k0 = min(max(q0 - radius, 0), L - wk)
            qsl.append(pl.ds(q0 * d + r, qrows, stride=d))
            ksl.append(pl.ds(k0 * d + r, wk, stride=d))
        qt = jnp.stack([dq_ref[0, sl, :].astype(BF16) for sl in qsl])
        kt = jnp.stack([dk_ref[0, sl, :].astype(BF16) for sl in ksl])
        vt = jnp.stack([dv_ref[0, sl, :].astype(BF16) for sl in ksl])
        bias = bias_ref[pl.ds(j * per, per)]
        heads = []
        for h in range(2):
            qh = jnp.where(head0 if h == 0 else ~head0, qt, jnp.zeros_like(qt))
            s = jnp.einsum("tqc,tkc->tqk", qh, kt, preferred_element_type=F32) + bias
            mx = jnp.max(s, axis=-1, keepdims=True)
            p = jnp.exp(s - mx)
            heads.append((p.astype(BF16), mx, jnp.sum(p, axis=-1, keepdims=True)))
        return pi, qsl, vt, heads

    def dilated_values(ctx):
        pi, qsl, vt, heads = ctx
        parts = [(jnp.einsum("tqk,tkc->tqc", p, vt, preferred_element_type=F32), mx, z)
                 for p, mx, z in heads]
        n, mx, z = (jnp.where(head0, a, b) for a, b in zip(*parts))
        for i, dst in enumerate(qsl):
            n_scr[pi, dst, :] = n[i]
            m_scr[pi, dst, :] = mx[i]
            z_scr[pi, dst, :] = z[i]

    mx0 = scores(0, 0)
    for j in range(nblocks):
        ctx = dilated_scores(j, 0)
        mx1 = scores(j, 1)
        dilated_values(ctx)
        ctx = dilated_scores(j, 1)
        o0 = attend(0, mx0)
        dilated_values(ctx)
        ctx = dilated_scores(j, 2)
        if j + 1 < nblocks:
            mx0 = scores(j + 1, 0)
        dilated_values(ctx)
        o1 = attend(1, mx1)
        finish(j, o0, o1)

    chunk = 256

    def combine(c, carry):
        sl = pl.ds(pl.multiple_of(c * chunk, chunk), chunk)
        ms = [m_scr[pi, sl, :] for pi in range(len(DIL_PATTERNS))]
        top = functools.reduce(jnp.maximum, ms)
        num = jnp.zeros((chunk, LANES), F32)
        den = jnp.zeros((chunk, LANES), F32)
        for pi in range(len(DIL_PATTERNS)):
            a = jnp.exp(ms[pi] - top)
            num = num + a * n_scr[pi, sl, :]
            den = den + a * z_scr[pi, sl, :]
        og_ref[0, sl, :] = (num / den).astype(og_ref.dtype)
        return carry

    lax.fori_loop(0, S // chunk, combine, 0)


def _attention(qt, kd, vt, dil_qkv, lq1, lk1, lq2, lk2, g_subln):
    B, S, _ = kd.shape
    nblk = D_DIL // LANES
    assert nblk == N_HEADS_DIFF
    small = lambda n: pl.BlockSpec((1, n), lambda b, h: (0, 0))
    biases = _dil_bias(S)
    const = lambda a: pl.BlockSpec(a.shape, lambda b, h: (0, 0, 0), pipeline_mode=pl.Buffered(1))
    s_scr = pltpu.VMEM((S, DIFF_Q_ROWS), F32)
    stat_scr = pltpu.VMEM((len(DIL_PATTERNS), S, LANES), F32)
    return pl.pallas_call(
        _attn_kernel,
        grid=(B, N_HEADS_DIFF),
        in_specs=[
            pl.BlockSpec((1, LANES, S), lambda b, h: (b, h, 0)),
            pl.BlockSpec((1, S, LANES), lambda b, h: (b, 0, h)),
            pl.BlockSpec((1, LANES, S), lambda b, h: (b, h, 0)),
            small(HEAD_DIM), small(HEAD_DIM), small(HEAD_DIM), small(HEAD_DIM),
            pl.BlockSpec((2 * HEAD_DIM, 1), lambda b, h: (0, 0)),
            pl.BlockSpec((1, S, LANES), lambda b, h: (b, 0, h)),
            pl.BlockSpec((1, S, LANES), lambda b, h: (b, 0, nblk + h)),
            pl.BlockSpec((1, S, LANES), lambda b, h: (b, 0, 2 * nblk + h)),
        ] + [const(a) for a in biases],
        out_specs=[
            pl.BlockSpec((1, S, LANES), lambda b, h: (b, 0, h)),
            pl.BlockSpec((1, S, LANES), lambda b, h: (b, 0, h)),
        ],
        out_shape=[
            jax.ShapeDtypeStruct((B, S, D_DIFF), BF16),
            jax.ShapeDtypeStruct((B, S, D_DIL), BF16),
        ],
        scratch_shapes=[s_scr, s_scr, stat_scr, stat_scr, stat_scr],
        compiler_params=pltpu.CompilerParams(
            dimension_semantics=("parallel", "parallel"),
            vmem_limit_bytes=VMEM_LIMIT_BYTES),
        name="attention",
    )(qt, kd, vt, lq1, lk1, lq2, lk2, g_subln.reshape(2 * HEAD_DIM, 1),
      dil_qkv, dil_qkv, dil_qkv, *biases)


def _post_kernel(x_ref, od_ref, og_ref, p_ref, wo_ref, gm_ref, wu_ref, wd_ref,
                 gp_ref, wg_ref, wp_ref, gf_ref, o_ref):
    mixed = jnp.concatenate([od_ref[...], og_ref[...]], axis=-1)
    h = x_ref[...] + jnp.dot(mixed, wo_ref[...], preferred_element_type=F32)

    hn = _rms(h, gm_ref[...], NORM_EPS).astype(BF16)
    for c in range(D_FF // FF_CHUNK):
        u = jnp.dot(hn, wu_ref[:, c * FF_CHUNK:(c + 1) * FF_CHUNK], preferred_element_type=F32)
        u = jnp.square(jnp.maximum(u, 0.0)).astype(BF16)
        h = h + jnp.dot(u, wd_ref[c * FF_CHUNK:(c + 1) * FF_CHUNK, :], preferred_element_type=F32)

    hn = _rms(h, gp_ref[...], NORM_EPS).astype(BF16)
    gate = jax.nn.sigmoid(jnp.dot(hn, wg_ref[...], preferred_element_type=F32))
    emb = jnp.dot(p_ref[...].astype(BF16), wp_ref[...], preferred_element_type=F32)
    h = h + gate * emb
    o_ref[...] = _rms(h, gf_ref[...], NORM_EPS)


def _post(x2, od2, og2, p2, w_out, g_mlp, w_up, w_down, g_ple, w_gate, w_ple, g_final):
    N, D = x2.shape
    rows = POST_ROWS
    tok = lambda width: pl.BlockSpec((rows, width), lambda i: (i, 0))
    whole = lambda a: pl.BlockSpec(a.shape, lambda i: (0, 0), pipeline_mode=pl.Buffered(1))
    return pl.pallas_call(
        _post_kernel,
        grid=(N // rows,),
        in_specs=[
            tok(D), tok(D_DIFF), tok(D_DIL), tok(D_PLE),
            whole(w_out), whole(g_mlp), whole(w_up), whole(w_down),
            whole(g_ple), whole(w_gate), whole(w_ple), whole(g_final),
        ],
        out_specs=tok(D),
        out_shape=jax.ShapeDtypeStruct((N, D), F32),
        compiler_params=pltpu.CompilerParams(
            dimension_semantics=("parallel",),
            vmem_limit_bytes=VMEM_LIMIT_BYTES),
        name="out_mlp_ple",
    )(x2, od2, og2, p2, w_out, g_mlp, w_up, w_down, g_ple, w_gate, w_ple, g_final)


def _rope_tables(seq):
    inv = ROPE_THETA ** (-jnp.arange(0, HEAD_DIM, 2, dtype=F32) / HEAD_DIM)
    ang = jnp.arange(seq, dtype=F32)[:, None] * inv[None, :]
    cos, sin = jnp.cos(ang), jnp.sin(ang)
    reps = LANES // HEAD_DIM
    cos_t = jnp.tile(jnp.concatenate([cos, cos], axis=-1), (1, reps))
    sin_t = jnp.tile(jnp.concatenate([-sin, sin], axis=-1), (1, reps))
    return cos_t, sin_t, cos.T, sin.T


def kernel(x, p, w_in, w_out, g_mix, lambda_q1, lambda_k1, lambda_q2, lambda_k2, g_subln,
           g_mlp, w_up, w_down, g_ple, w_ple_gate, w_ple_proj, g_final):
    B, S, D = x.shape
    assert x.shape[1:] == (2048, D_MODEL) and w_in.shape == (1, D_MODEL, 2 * D_GROUP)

    qt, kd, vt, dil_qkv = _project(x, g_mix, w_in[0], _rope_tables(S))
    od, og = _attention(qt, kd, vt, dil_qkv, lambda_q1, lambda_k1, lambda_q2, lambda_k2, g_subln)

    out = _post(
        x.reshape(B * S, D), od.reshape(B * S, D_DIFF), og.reshape(B * S, D_DIL),
        p[0].reshape(B * S, D_PLE),
        w_out[0].astype(BF16), g_mlp, w_up[0].astype(BF16), w_down[0].astype(BF16),
        g_ple, w_ple_gate[0].astype(BF16), w_ple_proj[0].astype(BF16), g_final.reshape(1, D))
    return out.reshape(B, S, D)
```

```python
import functools
import math

import jax
import jax.numpy as jnp
import numpy as np
from jax import lax
from jax.experimental import pallas as pl
from jax.experimental.pallas import tpu as pltpu

D_MODEL = 1024
HEAD_DIM = 64
D_DIFF = 512
D_DIL = 512
D_GROUP = 3 * D_DIFF
N_HEADS_DIFF = 4
DIL_PATTERNS = ((128, 1), (512, 4), (2048, 16))
D_FF = 4 * D_MODEL
D_PLE = 256
ROPE_THETA = 10000.0
NORM_EPS = 1e-6
SUBLN_EPS = 1e-5
NEG_INF = -1e30
LAM_INIT = 0.8 - 0.6 * math.exp(-0.3 * 0)
LOG2_E = math.log2(math.e)

LANES = 128
BF16_SUBLANES = 16
VMEM_LIMIT_BYTES = 56 * 1024 * 1024

PROJ_ROWS = 512
DIFF_Q_ROWS = 512
DIL_Q_ROWS = 128
POST_ROWS = 512
POST_HALVES = 2
FF_CHUNK = 1024

BF16 = jnp.bfloat16
F32 = jnp.float32
NT_DIMS = (((1,), (1,)), ((), ()))


def _rms(x, g, eps):
    return x * lax.rsqrt(jnp.mean(x * x, axis=-1, keepdims=True) + eps) * g


def _proj_kernel(x_ref, g_ref, wqt_ref, wvt_ref, w_ref, cos_ref, sin_ref, cost_ref, sint_ref,
                 qt_ref, kd_ref, vt_ref, dil_ref):
    x = x_ref[0]
    hn = _rms(x, g_ref[...], NORM_EPS).astype(BF16)
    cos = cos_ref[...]
    sin = sin_ref[...]
    lane = lax.broadcasted_iota(jnp.int32, cos.shape, 1)
    first_half = (lane % HEAD_DIM) < (HEAD_DIM // 2)
    scale = HEAD_DIM ** -0.5
    half = HEAD_DIM // 2

    def rope(t):
        swapped = jnp.where(first_half,
                            pltpu.roll(t, LANES - half, 1),
                            pltpu.roll(t, half, 1))
        return t * cos + swapped * sin

    qt = lax.dot_general(wqt_ref[...], hn, NT_DIMS, preferred_element_type=F32)
    cos_t = cost_ref[...] * (scale * LOG2_E)
    sin_t = sint_ref[...] * (scale * LOG2_E)
    for hb in range(D_DIFF // HEAD_DIM):
        t1 = qt[hb * HEAD_DIM:hb * HEAD_DIM + half]
        t2 = qt[hb * HEAD_DIM + half:(hb + 1) * HEAD_DIM]
        qt_ref[0, hb * HEAD_DIM:hb * HEAD_DIM + half, :] = (t1 * cos_t - t2 * sin_t).astype(BF16)
        qt_ref[0, hb * HEAD_DIM + half:(hb + 1) * HEAD_DIM, :] = (t1 * sin_t + t2 * cos_t).astype(BF16)

    vt_ref[0] = lax.dot_general(wvt_ref[...], hn, NT_DIMS, preferred_element_type=F32).astype(BF16)

    slabs = ((kd_ref, 0, True, False), (dil_ref, 0, True, True),
             (dil_ref, D_DIL, True, False), (dil_ref, 2 * D_DIL, False, False))
    for n, (out_ref, o0, use_rope, use_scale) in enumerate(slabs):
        acc = jnp.dot(hn, w_ref[:, n * D_DIFF:(n + 1) * D_DIFF], preferred_element_type=F32)
        for c in range(D_DIFF // LANES):
            t = acc[:, c * LANES:(c + 1) * LANES]
            if use_rope:
                t = rope(t)
            if use_scale:
                t = t * scale
            out_ref[0, :, o0 + c * LANES:o0 + (c + 1) * LANES] = t.astype(out_ref.dtype)


def _project(x, g_mix, w_in, tables):
    B, S, D = x.shape
    rows = PROJ_ROWS
    cos_t, sin_t, cos_tt, sin_tt = tables
    wqt = w_in[:, :D_DIFF].T.astype(BF16)
    wvt = w_in[:, 2 * D_DIFF:3 * D_DIFF].T.astype(BF16)
    w_rest = jnp.concatenate([w_in[:, D_DIFF:2 * D_DIFF], w_in[:, D_GROUP:]], axis=1).astype(BF16)
    const = lambda a: pl.BlockSpec(a.shape, lambda b, i: (0, 0), pipeline_mode=pl.Buffered(1))
    return pl.pallas_call(
        _proj_kernel,
        grid=(B, S // rows),
        in_specs=[
            pl.BlockSpec((1, rows, D), lambda b, i: (b, i, 0)),
            pl.BlockSpec((1, D), lambda b, i: (0, 0)),
            const(wqt), const(wvt), const(w_rest),
            pl.BlockSpec((rows, LANES), lambda b, i: (i, 0)),
            pl.BlockSpec((rows, LANES), lambda b, i: (i, 0)),
            pl.BlockSpec((HEAD_DIM // 2, rows), lambda b, i: (0, i)),
            pl.BlockSpec((HEAD_DIM // 2, rows), lambda b, i: (0, i)),
        ],
        out_specs=[
            pl.BlockSpec((1, D_DIFF, rows), lambda b, i: (b, 0, i)),
            pl.BlockSpec((1, rows, D_DIFF), lambda b, i: (b, i, 0)),
            pl.BlockSpec((1, D_DIFF, rows), lambda b, i: (b, 0, i)),
            pl.BlockSpec((1, rows, D_GROUP), lambda b, i: (b, i, 0)),
        ],
        out_shape=[
            jax.ShapeDtypeStruct((B, D_DIFF, S), BF16),
            jax.ShapeDtypeStruct((B, S, D_DIFF), BF16),
            jax.ShapeDtypeStruct((B, D_DIFF, S), BF16),
            jax.ShapeDtypeStruct((B, S, D_GROUP), F32),
        ],
        compiler_params=pltpu.CompilerParams(
            dimension_semantics=("parallel", "parallel"),
            vmem_limit_bytes=VMEM_LIMIT_BYTES),
        name="proj_rope",
    )(x, g_mix, wqt, wvt, w_rest, cos_t, sin_t, cos_tt, sin_tt)


def _dil_geometry(window, d, seq):
    radius, L = window // (2 * d), seq // d
    wk = min(DIL_Q_ROWS + 2 * radius, L)
    return radius, L, wk, L // DIL_Q_ROWS


def _dil_bias(seq):
    out = []
    for window, d in DIL_PATTERNS:
        radius, L, wk, blocks = _dil_geometry(window, d, seq)
        row = np.arange(DIL_Q_ROWS)[:, None]
        col = np.arange(wk)[None, :]
        bias = []
        for t in range(d * blocks):
            q0 = (t % blocks) * DIL_Q_ROWS
            k0 = min(max(q0 - radius, 0), L - wk)
            bias.append(np.where(np.abs((k0 + col) - (q0 + row)) <= radius, 0.0, NEG_INF))
        out.append(jnp.asarray(np.stack(bias), dtype=F32))
    return out


def _attn_kernel(qt_ref, k_ref, vt_ref, lq1_ref, lk1_ref, lq2_ref, lk2_ref, g_ref,
                 dq_ref, dk_ref, dv_ref, b0_ref, b1_ref, b2_ref,
                 od_ref, og_ref, s0_scr, s1_scr, n_scr, m_scr, z_scr):
    S = k_ref.shape[1]
    rows = DIFF_Q_ROWS
    nblocks = S // rows
    k = k_ref[0]
    vt_ones = jnp.concatenate([vt_ref[0], jnp.ones((BF16_SUBLANES, S), BF16)], axis=0)
    lam = (jnp.exp(jnp.sum(lq1_ref[...] * lk1_ref[...], axis=-1, keepdims=True))
           - jnp.exp(jnp.sum(lq2_ref[...] * lk2_ref[...], axis=-1, keepdims=True))
           + LAM_INIT)
    gain = g_ref[...] * (1.0 - LAM_INIT)
    feat = lax.broadcasted_iota(jnp.int32, (2 * HEAD_DIM, rows), 0)
    s_scr = (s0_scr, s1_scr)

    def cols(j):
        return pl.ds(j * rows, rows)

    def scores(j, m):
        qt = qt_ref[0, :, cols(j)]
        in_map = (feat >= m * HEAD_DIM) & (feat < (m + 1) * HEAD_DIM)
        qm = jnp.where(in_map, qt, jnp.zeros_like(qt))
        s = jnp.dot(k, qm, preferred_element_type=F32)
        s_scr[m][...] = s
        return jnp.max(s, axis=0, keepdims=True)

    def attend(m, mx):
        p = jnp.exp2(s_scr[m][...] - mx).astype(BF16)
        o = jnp.dot(vt_ones, p, preferred_element_type=F32)
        return o[:2 * HEAD_DIM] / o[2 * HEAD_DIM:2 * HEAD_DIM + 1]

    def finish(j, o0, o1):
        o = o0 - lam * o1
        o = o * lax.rsqrt(jnp.mean(o * o, axis=0, keepdims=True) + SUBLN_EPS) * gain
        od_ref[0, cols(j), :] = o.T.astype(od_ref.dtype)

    qrows = DIL_Q_ROWS
    head0 = lax.broadcasted_iota(jnp.int32, (1, 1, LANES), 2) < HEAD_DIM

    def dilated_scores(j, pi):
        (window, d), bias_ref = DIL_PATTERNS[pi], (b0_ref, b1_ref, b2_ref)[pi]
        radius, L, wk, blocks = _dil_geometry(window, d, S)
        per = d * blocks // nblocks
        qsl, ksl = [], []
        for i in range(per):
            t = j * per + i
            r, q0 = t // blocks, (t % blocks) * qrows
            k0 = min(max(q0 - radius, 0), L - wk)
            qsl.append(pl.ds(q0 * d + r, qrows, stride=d))
            ksl.append(pl.ds(k0 * d + r, wk, stride=d))
        qt = jnp.stack([dq_ref[0, sl, :].astype(BF16) for sl in qsl])
        kt = jnp.stack([dk_ref[0, sl, :].astype(BF16) for sl in ksl])
        vt = jnp.stack([dv_ref[0, sl, :].astype(BF16) for sl in ksl])
        bias = bias_ref[pl.ds(j * per, per)]
        heads = []
        for h in range(2):
            qh = jnp.where(head0 if h == 0 else ~head0, qt, jnp.zeros_like(qt))
            s = jnp.einsum("tqc,tkc->tqk", qh, kt, preferred_element_type=F32) + bias
            mx = jnp.max(s, axis=-1, keepdims=True)
            p = jnp.exp(s - mx)
            heads.append((p.astype(BF16), mx, jnp.sum(p, axis=-1, keepdims=True)))
        return pi, qsl, vt, heads

    def dilated_values(ctx):
        pi, qsl, vt, heads = ctx
        parts = [(jnp.einsum("tqk,tkc->tqc", p, vt, preferred_element_type=F32), mx, z)
                 for p, mx, z in heads]
        n, mx, z = (jnp.where(head0, a, b) for a, b in zip(*parts))
        for i, dst in enumerate(qsl):
            n_scr[pi, dst, :] = n[i]
            m_scr[pi, dst, :] = mx[i]
            z_scr[pi, dst, :] = z[i]

    mx0 = scores(0, 0)
    for j in range(nblocks):
        ctx = dilated_scores(j, 0)
        mx1 = scores(j, 1)
        dilated_values(ctx)
        ctx = dilated_scores(j, 1)
        o0 = attend(0, mx0)
        dilated_values(ctx)
        ctx = dilated_scores(j, 2)
        if j + 1 < nblocks:
            mx0 = scores(j + 1, 0)
        dilated_values(ctx)
        o1 = attend(1, mx1)
        finish(j, o0, o1)

    chunk = 256

    def combine(c, carry):
        sl = pl.ds(pl.multiple_of(c * chunk, chunk), chunk)
        ms = [m_scr[pi, sl, :] for pi in range(len(DIL_PATTERNS))]
        top = functools.reduce(jnp.maximum, ms)
        num = jnp.zeros((chunk, LANES), F32)
        den = jnp.zeros((chunk, LANES), F32)
        for pi in range(len(DIL_PATTERNS)):
            a = jnp.exp(ms[pi] - top)
            num = num + a * n_scr[pi, sl, :]
            den = den + a * z_scr[pi, sl, :]
        og_ref[0, sl, :] = (num / den).astype(og_ref.dtype)
        return carry

    lax.fori_loop(0, S // chunk, combine, 0)


def _attention(qt, kd, vt, dil_qkv, lq1, lk1, lq2, lk2, g_subln):
    B, S, _ = kd.shape
    nblk = D_DIL // LANES
    assert nblk == N_HEADS_DIFF
    small = lambda n: pl.BlockSpec((1, n), lambda b, h: (0, 0))
    biases = _dil_bias(S)
    const = lambda a: pl.BlockSpec(a.shape, lambda b, h: (0, 0, 0), pipeline_mode=pl.Buffered(1))
    s_scr = pltpu.VMEM((S, DIFF_Q_ROWS), F32)
    stat_scr = pltpu.VMEM((len(DIL_PATTERNS), S, LANES), F32)
    return pl.pallas_call(
        _attn_kernel,
        grid=(B, N_HEADS_DIFF),
        in_specs=[
            pl.BlockSpec((1, LANES, S), lambda b, h: (b, h, 0)),
            pl.BlockSpec((1, S, LANES), lambda b, h: (b, 0, h)),
            pl.BlockSpec((1, LANES, S), lambda b, h: (b, h, 0)),
            small(HEAD_DIM), small(HEAD_DIM), small(HEAD_DIM), small(HEAD_DIM),
            pl.BlockSpec((2 * HEAD_DIM, 1), lambda b, h: (0, 0)),
            pl.BlockSpec((1, S, LANES), lambda b, h: (b, 0, h)),
            pl.BlockSpec((1, S, LANES), lambda b, h: (b, 0, nblk + h)),
            pl.BlockSpec((1, S, LANES), lambda b, h: (b, 0, 2 * nblk + h)),
        ] + [const(a) for a in biases],
        out_specs=[
            pl.BlockSpec((1, S, LANES), lambda b, h: (b, 0, h)),
            pl.BlockSpec((1, S, LANES), lambda b, h: (b, 0, h)),
        ],
        out_shape=[
            jax.ShapeDtypeStruct((B, S, D_DIFF), BF16),
            jax.ShapeDtypeStruct((B, S, D_DIL), BF16),
        ],
        scratch_shapes=[s_scr, s_scr, stat_scr, stat_scr, stat_scr],
        compiler_params=pltpu.CompilerParams(
            dimension_semantics=("parallel", "parallel"),
            vmem_limit_bytes=VMEM_LIMIT_BYTES),
        name="attention",
    )(qt, kd, vt, lq1, lk1, lq2, lk2, g_subln.reshape(2 * HEAD_DIM, 1),
      dil_qkv, dil_qkv, dil_qkv, *biases)


def _post_kernel(x_ref, od_ref, og_ref, p_ref, wo_ref, gm_ref, wu_ref, wd_ref,
                 gp_ref, wg_ref, wp_ref, gf_ref, o_ref):
    half_rows = x_ref.shape[0] // POST_HALVES
    halves = [pl.ds(i * half_rows, half_rows) for i in range(POST_HALVES)]

    h = []
    for sl in halves:
        mixed = jnp.concatenate([od_ref[sl, :], og_ref[sl, :]], axis=-1)
        h.append(x_ref[sl, :] + jnp.dot(mixed, wo_ref[...], preferred_element_type=F32))

    for i in range(POST_HALVES):
        hn = _rms(h[i], gm_ref[...], NORM_EPS).astype(BF16)
        for c in range(D_FF // FF_CHUNK):
            u = jnp.dot(hn, wu_ref[:, c * FF_CHUNK:(c + 1) * FF_CHUNK], preferred_element_type=F32)
            u = jnp.square(jnp.maximum(u, 0.0)).astype(BF16)
            h[i] = h[i] + jnp.dot(u, wd_ref[c * FF_CHUNK:(c + 1) * FF_CHUNK, :],
                                  preferred_element_type=F32)

    for i, sl in enumerate(halves):
        hn = _rms(h[i], gp_ref[...], NORM_EPS).astype(BF16)
        gate = jax.nn.sigmoid(jnp.dot(hn, wg_ref[...], preferred_element_type=F32))
        emb = jnp.dot(p_ref[sl, :].astype(BF16), wp_ref[...], preferred_element_type=F32)
        o_ref[sl, :] = _rms(h[i] + gate * emb, gf_ref[...], NORM_EPS)


def _post(x2, od2, og2, p2, w_out, g_mlp, w_up, w_down, g_ple, w_gate, w_ple, g_final):
    N, D = x2.shape
    rows = POST_ROWS
    tok = lambda width: pl.BlockSpec((rows, width), lambda i: (i, 0))
    whole = lambda a: pl.BlockSpec(a.shape, lambda i: (0, 0), pipeline_mode=pl.Buffered(1))
    return pl.pallas_call(
        _post_kernel,
        grid=(N // rows,),
        in_specs=[
            tok(D), tok(D_DIFF), tok(D_DIL), tok(D_PLE),
            whole(w_out), whole(g_mlp), whole(w_up), whole(w_down),
            whole(g_ple), whole(w_gate), whole(w_ple), whole(g_final),
        ],
        out_specs=tok(D),
        out_shape=jax.ShapeDtypeStruct((N, D), F32),
        compiler_params=pltpu.CompilerParams(
            dimension_semantics=("parallel",),
            vmem_limit_bytes=VMEM_LIMIT_BYTES),
        name="out_mlp_ple",
    )(x2, od2, og2, p2, w_out, g_mlp, w_up, w_down, g_ple, w_gate, w_ple, g_final)


def _rope_tables(seq):
    inv = ROPE_THETA ** (-jnp.arange(0, HEAD_DIM, 2, dtype=F32) / HEAD_DIM)
    ang = jnp.arange(seq, dtype=F32)[:, None] * inv[None, :]
    cos, sin = jnp.cos(ang), jnp.sin(ang)
    reps = LANES // HEAD_DIM
    cos_t = jnp.tile(jnp.concatenate([cos, cos], axis=-1), (1, reps))
    sin_t = jnp.tile(jnp.concatenate([-sin, sin], axis=-1), (1, reps))
    return cos_t, sin_t, cos.T, sin.T


def kernel(x, p, w_in, w_out, g_mix, lambda_q1, lambda_k1, lambda_q2, lambda_k2, g_subln,
           g_mlp, w_up, w_down, g_ple, w_ple_gate, w_ple_proj, g_final):
    B, S, D = x.shape
    assert x.shape[1:] == (2048, D_MODEL) and w_in.shape == (1, D_MODEL, 2 * D_GROUP)

    qt, kd, vt, dil_qkv = _project(x, g_mix, w_in[0], _rope_tables(S))
    od, og = _attention(qt, kd, vt, dil_qkv, lambda_q1, lambda_k1, lambda_q2, lambda_k2, g_subln)

    out = _post(
        x.reshape(B * S, D), od.reshape(B * S, D_DIFF), og.reshape(B * S, D_DIL),
        p[0].reshape(B * S, D_PLE),
        w_out[0].astype(BF16), g_mlp, w_up[0].astype(BF16), w_down[0].astype(BF16),
        g_ple, w_ple_gate[0].astype(BF16), w_ple_proj[0].astype(BF16), g_final.reshape(1, D))
    return out.reshape(B, S, D)
```

```python
import functools
import math

import jax
import jax.numpy as jnp
import numpy as np
from jax import lax
from jax.experimental import pallas as pl
from jax.experimental.pallas import tpu as pltpu

D_MODEL = 1024
HEAD_DIM = 64
D_DIFF = 512
D_DIL = 512
D_GROUP = 3 * D_DIFF
N_HEADS_DIFF = 4
DIL_PATTERNS = ((128, 1), (512, 4), (2048, 16))
D_FF = 4 * D_MODEL
D_PLE = 256
ROPE_THETA = 10000.0
NORM_EPS = 1e-6
SUBLN_EPS = 1e-5
NEG_INF = -1e30
LAM_INIT = 0.8 - 0.6 * math.exp(-0.3 * 0)
LOG2_E = math.log2(math.e)

LANES = 128
BF16_SUBLANES = 16
VMEM_LIMIT_BYTES = 56 * 1024 * 1024

PROJ_ROWS = 1024
PROJ_HALVES = 2
DIFF_Q_ROWS = 512
DIL_Q_ROWS = 128
POST_ROWS = 512
POST_HALVES = 2
FF_CHUNK = 1024

BF16 = jnp.bfloat16
F32 = jnp.float32
NT_DIMS = (((1,), (1,)), ((), ()))


def _rms(x, g, eps):
    return x * lax.rsqrt(jnp.mean(x * x, axis=-1, keepdims=True) + eps) * g


def _proj_kernel(x_ref, g_ref, wqt_ref, wvt_ref, w_ref, cos_ref, sin_ref, cost_ref, sint_ref,
                 qt_ref, kd_ref, vt_ref, dil_ref):
    scale = HEAD_DIM ** -0.5
    half = HEAD_DIM // 2
    sub_rows = x_ref.shape[1] // PROJ_HALVES
    lane = lax.broadcasted_iota(jnp.int32, (sub_rows, LANES), 1)
    first_half = (lane % HEAD_DIM) < half
    slabs = ((kd_ref, 0, True, False), (dil_ref, 0, True, True),
             (dil_ref, D_DIL, True, False), (dil_ref, 2 * D_DIL, False, False))

    for i in range(PROJ_HALVES):
        sl = pl.ds(i * sub_rows, sub_rows)
        hn = _rms(x_ref[0, sl, :], g_ref[...], NORM_EPS).astype(BF16)
        cos = cos_ref[sl, :]
        sin = sin_ref[sl, :]

        def rope(t, cos=cos, sin=sin):
            swapped = jnp.where(first_half,
                                pltpu.roll(t, LANES - half, 1),
                                pltpu.roll(t, half, 1))
            return t * cos + swapped * sin

        qt = lax.dot_general(wqt_ref[...], hn, NT_DIMS, preferred_element_type=F32)
        cos_t = cost_ref[:, sl] * (scale * LOG2_E)
        sin_t = sint_ref[:, sl] * (scale * LOG2_E)
        for hb in range(D_DIFF // HEAD_DIM):
            t1 = qt[hb * HEAD_DIM:hb * HEAD_DIM + half]
            t2 = qt[hb * HEAD_DIM + half:(hb + 1) * HEAD_DIM]
            qt_ref[0, hb * HEAD_DIM:hb * HEAD_DIM + half, sl] = (t1 * cos_t - t2 * sin_t).astype(BF16)
            qt_ref[0, hb * HEAD_DIM + half:(hb + 1) * HEAD_DIM, sl] = (t1 * sin_t + t2 * cos_t).astype(BF16)

        vt_ref[0, :, sl] = lax.dot_general(wvt_ref[...], hn, NT_DIMS,
                                           preferred_element_type=F32).astype(BF16)

        for n, (out_ref, o0, use_rope, use_scale) in enumerate(slabs):
            acc = jnp.dot(hn, w_ref[:, n * D_DIFF:(n + 1) * D_DIFF], preferred_element_type=F32)
            for c in range(D_DIFF // LANES):
                t = acc[:, c * LANES:(c + 1) * LANES]
                if use_rope:
                    t = rope(t)
                if use_scale:
                    t = t * scale
                out_ref[0, sl, o0 + c * LANES:o0 + (c + 1) * LANES] = t.astype(out_ref.dtype)


def _project(x, g_mix, w_in, tables):
    B, S, D = x.shape
    rows = PROJ_ROWS
    cos_t, sin_t, cos_tt, sin_tt = tables
    wqt = w_in[:, :D_DIFF].T.astype(BF16)
    wvt = w_in[:, 2 * D_DIFF:3 * D_DIFF].T.astype(BF16)
    w_rest = jnp.concatenate([w_in[:, D_DIFF:2 * D_DIFF], w_in[:, D_GROUP:]], axis=1).astype(BF16)
    const = lambda a: pl.BlockSpec(a.shape, lambda b, i: (0, 0), pipeline_mode=pl.Buffered(1))
    return pl.pallas_call(
        _proj_kernel,
        grid=(B, S // rows),
        in_specs=[
            pl.BlockSpec((1, rows, D), lambda b, i: (b, i, 0)),
            pl.BlockSpec((1, D), lambda b, i: (0, 0)),
            const(wqt), const(wvt), const(w_rest),
            pl.BlockSpec((rows, LANES), lambda b, i: (i, 0)),
            pl.BlockSpec((rows, LANES), lambda b, i: (i, 0)),
            pl.BlockSpec((HEAD_DIM // 2, rows), lambda b, i: (0, i)),
            pl.BlockSpec((HEAD_DIM // 2, rows), lambda b, i: (0, i)),
        ],
        out_specs=[
            pl.BlockSpec((1, D_DIFF, rows), lambda b, i: (b, 0, i)),
            pl.BlockSpec((1, rows, D_DIFF), lambda b, i: (b, i, 0)),
            pl.BlockSpec((1, D_DIFF, rows), lambda b, i: (b, 0, i)),
            pl.BlockSpec((1, rows, D_GROUP), lambda b, i: (b, i, 0)),
        ],
        out_shape=[
            jax.ShapeDtypeStruct((B, D_DIFF, S), BF16),
            jax.ShapeDtypeStruct((B, S, D_DIFF), BF16),
            jax.ShapeDtypeStruct((B, D_DIFF, S), BF16),
            jax.ShapeDtypeStruct((B, S, D_GROUP), F32),
        ],
        compiler_params=pltpu.CompilerParams(
            dimension_semantics=("parallel", "parallel"),
            vmem_limit_bytes=VMEM_LIMIT_BYTES),
        name="proj_rope",
    )(x, g_mix, wqt, wvt, w_rest, cos_t, sin_t, cos_tt, sin_tt)


def _dil_geometry(window, d, seq):
    radius, L = window // (2 * d), seq // d
    wk = min(DIL_Q_ROWS + 2 * radius, L)
    return radius, L, wk, L // DIL_Q_ROWS


def _dil_bias(seq):
    out = []
    for window, d in DIL_PATTERNS:
        radius, L, wk, blocks = _dil_geometry(window, d, seq)
        row = np.arange(DIL_Q_ROWS)[:, None]
        col = np.arange(wk)[None, :]
        bias = []
        for t in range(d * blocks):
            q0 = (t % blocks) * DIL_Q_ROWS
            k0 = min(max(q0 - radius, 0), L - wk)
            bias.append(np.where(np.abs((k0 + col) - (q0 + row)) <= radius, 0.0, NEG_INF))
        out.append(jnp.asarray(np.stack(bias), dtype=F32))
    return out


def _attn_kernel(qt_ref, k_ref, vt_ref, lq1_ref, lk1_ref, lq2_ref, lk2_ref, g_ref,
                 dq_ref, dk_ref, dv_ref, b0_ref, b1_ref, b2_ref,
                 od_ref, og_ref, s0_scr, s1_scr, n_scr, m_scr, z_scr):
    S = k_ref.shape[1]
    rows = DIFF_Q_ROWS
    nblocks = S // rows
    k = k_ref[0]
    vt_ones = jnp.concatenate([vt_ref[0], jnp.ones((BF16_SUBLANES, S), BF16)], axis=0)
    lam = (jnp.exp(jnp.sum(lq1_ref[...] * lk1_ref[...], axis=-1, keepdims=True))
           - jnp.exp(jnp.sum(lq2_ref[...] * lk2_ref[...], axis=-1, keepdims=True))
           + LAM_INIT)
    gain = g_ref[...] * (1.0 - LAM_INIT)
    feat = lax.broadcasted_iota(jnp.int32, (2 * HEAD_DIM, rows), 0)
    s_scr = (s0_scr, s1_scr)

    def cols(j):
        return pl.ds(j * rows, rows)

    def scores(j, m):
        qt = qt_ref[0, :, cols(j)]
        in_map = (feat >= m * HEAD_DIM) & (feat < (m + 1) * HEAD_DIM)
        qm = jnp.where(in_map, qt, jnp.zeros_like(qt))
        s = jnp.dot(k, qm, preferred_element_type=F32)
        s_scr[m][...] = s
        return jnp.max(s, axis=0, keepdims=True)

    def attend(m, mx):
        p = jnp.exp2(s_scr[m][...] - mx).astype(BF16)
        o = jnp.dot(vt_ones, p, preferred_element_type=F32)
        return o[:2 * HEAD_DIM] / o[2 * HEAD_DIM:2 * HEAD_DIM + 1]

    def finish(j, o0, o1):
        o = o0 - lam * o1
        o = o * lax.rsqrt(jnp.mean(o * o, axis=0, keepdims=True) + SUBLN_EPS) * gain
        od_ref[0, cols(j), :] = o.T.astype(od_ref.dtype)

    qrows = DIL_Q_ROWS
    head0 = lax.broadcasted_iota(jnp.int32, (1, 1, LANES), 2) < HEAD_DIM

    def dilated_scores(j, pi):
        (window, d), bias_ref = DIL_PATTERNS[pi], (b0_ref, b1_ref, b2_ref)[pi]
        radius, L, wk, blocks = _dil_geometry(window, d, S)
        per = d * blocks // nblocks
        qsl, ksl = [], []
        for i in range(per):
            t = j * per + i
            r, q0 = t // blocks, (t % blocks) * qrows
            k0 = min(max(q0 - radius, 0), L - wk)
            qsl.append(pl.ds(q0 * d + r, qrows, stride=d))
            ksl.append(pl.ds(k0 * d + r, wk, stride=d))
        qt = jnp.stack([dq_ref[0, sl, :].astype(BF16) for sl in qsl])
        kt = jnp.stack([dk_ref[0, sl, :].astype(BF16) for sl in ksl])
        vt = jnp.stack([dv_ref[0, sl, :].astype(BF16) for sl in ksl])
        bias = bias_ref[pl.ds(j * per, per)]
        heads = []
        for h in range(2):
            qh = jnp.where(head0 if h == 0 else ~head0, qt, jnp.zeros_like(qt))
            s = jnp.einsum("tqc,tkc->tqk", qh, kt, preferred_element_type=F32) + bias
            mx = jnp.max(s, axis=-1, keepdims=True)
            p = jnp.exp(s - mx)
            heads.append((p.astype(BF16), mx, jnp.sum(p, axis=-1, keepdims=True)))
        return pi, qsl, vt, heads

    def dilated_values(ctx):
        pi, qsl, vt, heads = ctx
        parts = [(jnp.einsum("tqk,tkc->tqc", p, vt, preferred_element_type=F32), mx, z)
                 for p, mx, z in heads]
        n, mx, z = (jnp.where(head0, a, b) for a, b in zip(*parts))
        for i, dst in enumerate(qsl):
            n_scr[pi, dst, :] = n[i]
            m_scr[pi, dst, :] = mx[i]
            z_scr[pi, dst, :] = z[i]

    mx0 = scores(0, 0)
    for j in range(nblocks):
        ctx = dilated_scores(j, 0)
        mx1 = scores(j, 1)
        dilated_values(ctx)
        ctx = dilated_scores(j, 1)
        o0 = attend(0, mx0)
        dilated_values(ctx)
        ctx = dilated_scores(j, 2)
        if j + 1 < nblocks:
            mx0 = scores(j + 1, 0)
        dilated_values(ctx)
        o1 = attend(1, mx1)
        finish(j, o0, o1)

    chunk = 256

    def combine(c, carry):
        sl = pl.ds(pl.multiple_of(c * chunk, chunk), chunk)
        ms = [m_scr[pi, sl, :] for pi in range(len(DIL_PATTERNS))]
        top = functools.reduce(jnp.maximum, ms)
        num = jnp.zeros((chunk, LANES), F32)
        den = jnp.zeros((chunk, LANES), F32)
        for pi in range(len(DIL_PATTERNS)):
            a = jnp.exp(ms[pi] - top)
            num = num + a * n_scr[pi, sl, :]
            den = den + a * z_scr[pi, sl, :]
        og_ref[0, sl, :] = (num / den).astype(og_ref.dtype)
        return carry

    lax.fori_loop(0, S // chunk, combine, 0)


def _attention(qt, kd, vt, dil_qkv, lq1, lk1, lq2, lk2, g_subln):
    B, S, _ = kd.shape
    nblk = D_DIL // LANES
    assert nblk == N_HEADS_DIFF
    small = lambda n: pl.BlockSpec((1, n), lambda b, h: (0, 0))
    biases = _dil_bias(S)
    const = lambda a: pl.BlockSpec(a.shape, lambda b, h: (0, 0, 0), pipeline_mode=pl.Buffered(1))
    s_scr = pltpu.VMEM((S, DIFF_Q_ROWS), F32)
    stat_scr = pltpu.VMEM((len(DIL_PATTERNS), S, LANES), F32)
    return pl.pallas_call(
        _attn_kernel,
        grid=(B, N_HEADS_DIFF),
        in_specs=[
            pl.BlockSpec((1, LANES, S), lambda b, h: (b, h, 0)),
            pl.BlockSpec((1, S, LANES), lambda b, h: (b, 0, h)),
            pl.BlockSpec((1, LANES, S), lambda b, h: (b, h, 0)),
            small(HEAD_DIM), small(HEAD_DIM), small(HEAD_DIM), small(HEAD_DIM),
            pl.BlockSpec((2 * HEAD_DIM, 1), lambda b, h: (0, 0)),
            pl.BlockSpec((1, S, LANES), lambda b, h: (b, 0, h)),
            pl.BlockSpec((1, S, LANES), lambda b, h: (b, 0, nblk + h)),
            pl.BlockSpec((1, S, LANES), lambda b, h: (b, 0, 2 * nblk + h)),
        ] + [const(a) for a in biases],
        out_specs=[
            pl.BlockSpec((1, S, LANES), lambda b, h: (b, 0, h)),
            pl.BlockSpec((1, S, LANES), lambda b, h: (b, 0, h)),
        ],
        out_shape=[
            jax.ShapeDtypeStruct((B, S, D_DIFF), BF16),
            jax.ShapeDtypeStruct((B, S, D_DIL), BF16),
        ],
        scratch_shapes=[s_scr, s_scr, stat_scr, stat_scr, stat_scr],
        compiler_params=pltpu.CompilerParams(
            dimension_semantics=("parallel", "parallel"),
            vmem_limit_bytes=VMEM_LIMIT_BYTES),
        name="attention",
    )(qt, kd, vt, lq1, lk1, lq2, lk2, g_subln.reshape(2 * HEAD_DIM, 1),
      dil_qkv, dil_qkv, dil_qkv, *biases)


def _post_kernel(x_ref, od_ref, og_ref, p_ref, wo_ref, gm_ref, wu_ref, wd_ref,
                 gp_ref, wg_ref, wp_ref, gf_ref, o_ref):
    half_rows = x_ref.shape[0] // POST_HALVES
    halves = [pl.ds(i * half_rows, half_rows) for i in range(POST_HALVES)]

    h = []
    for sl in halves:
        mixed = jnp.concatenate([od_ref[sl, :], og_ref[sl, :]], axis=-1)
        h.append(x_ref[sl, :] + jnp.dot(mixed, wo_ref[...], preferred_element_type=F32))

    for i in range(POST_HALVES):
        hn = _rms(h[i], gm_ref[...], NORM_EPS).astype(BF16)
        for c in range(D_FF // FF_CHUNK):
            u = jnp.dot(hn, wu_ref[:, c * FF_CHUNK:(c + 1) * FF_CHUNK], preferred_element_type=F32)
            u = jnp.square(jnp.maximum(u, 0.0)).astype(BF16)
            h[i] = h[i] + jnp.dot(u, wd_ref[c * FF_CHUNK:(c + 1) * FF_CHUNK, :],
                                  preferred_element_type=F32)

    for i, sl in enumerate(halves):
        hn = _rms(h[i], gp_ref[...], NORM_EPS).astype(BF16)
        gate = jax.nn.sigmoid(jnp.dot(hn, wg_ref[...], preferred_element_type=F32))
        emb = jnp.dot(p_ref[sl, :].astype(BF16), wp_ref[...], preferred_element_type=F32)
        o_ref[sl, :] = _rms(h[i] + gate * emb, gf_ref[...], NORM_EPS)


def _post(x2, od2, og2, p2, w_out, g_mlp, w_up, w_down, g_ple, w_gate, w_ple, g_final):
    N, D = x2.shape
    rows = POST_ROWS
    tok = lambda width: pl.BlockSpec((rows, width), lambda i: (i, 0))
    whole = lambda a: pl.BlockSpec(a.shape, lambda i: (0, 0), pipeline_mode=pl.Buffered(1))
    return pl.pallas_call(
        _post_kernel,
        grid=(N // rows,),
        in_specs=[
            tok(D), tok(D_DIFF), tok(D_DIL), tok(D_PLE),
            whole(w_out), whole(g_mlp), whole(w_up), whole(w_down),
            whole(g_ple), whole(w_gate), whole(w_ple), whole(g_final),
        ],
        out_specs=tok(D),
        out_shape=jax.ShapeDtypeStruct((N, D), F32),
        compiler_params=pltpu.CompilerParams(
            dimension_semantics=("parallel",),
            vmem_limit_bytes=VMEM_LIMIT_BYTES),
        name="out_mlp_ple",
    )(x2, od2, og2, p2, w_out, g_mlp, w_up, w_down, g_ple, w_gate, w_ple, g_final)


def _rope_tables(seq):
    inv = ROPE_THETA ** (-jnp.arange(0, HEAD_DIM, 2, dtype=F32) / HEAD_DIM)
    ang = jnp.arange(seq, dtype=F32)[:, None] * inv[None, :]
    cos, sin = jnp.cos(ang), jnp.sin(ang)
    reps = LANES // HEAD_DIM
    cos_t = jnp.tile(jnp.concatenate([cos, cos], axis=-1), (1, reps))
    sin_t = jnp.tile(jnp.concatenate([-sin, sin], axis=-1), (1, reps))
    return cos_t, sin_t, cos.T, sin.T


def kernel(x, p, w_in, w_out, g_mix, lambda_q1, lambda_k1, lambda_q2, lambda_k2, g_subln,
           g_mlp, w_up, w_down, g_ple, w_ple_gate, w_ple_proj, g_final):
    B, S, D = x.shape
    assert x.shape[1:] == (2048, D_MODEL) and w_in.shape == (1, D_MODEL, 2 * D_GROUP)

    qt, kd, vt, dil_qkv = _project(x, g_mix, w_in[0], _rope_tables(S))
    od, og = _attention(qt, kd, vt, dil_qkv, lambda_q1, lambda_k1, lambda_q2, lambda_k2, g_subln)

    out = _post(
        x.reshape(B * S, D), od.reshape(B * S, D_DIFF), og.reshape(B * S, D_DIL),
        p[0].reshape(B * S, D_PLE),
        w_out[0].astype(BF16), g_mlp, w_up[0].astype(BF16), w_down[0].astype(BF16),
        g_ple, w_ple_gate[0].astype(BF16), w_ple_proj[0].astype(BF16), g_final.reshape(1, D))
    return out.reshape(B, S, D)
```

```python
import functools
import math

import jax
import jax.numpy as jnp
import numpy as np
from jax import lax
from jax.experimental import pallas as pl
from jax.experimental.pallas import tpu as pltpu

D_MODEL = 1024
HEAD_DIM = 64
D_DIFF = 512
D_DIL = 512
D_GROUP = 3 * D_DIFF
N_HEADS_DIFF = 4
DIL_PATTERNS = ((128, 1), (512, 4), (2048, 16))
D_FF = 4 * D_MODEL
D_PLE = 256
ROPE_THETA = 10000.0
NORM_EPS = 1e-6
SUBLN_EPS = 1e-5
NEG_INF = -1e30
LAM_INIT = 0.8 - 0.6 * math.exp(-0.3 * 0)
LOG2_E = math.log2(math.e)

LANES = 128
BF16_SUBLANES = 16
VMEM_LIMIT_BYTES = 56 * 1024 * 1024

PROJ_ROWS = 1024
PROJ_HALVES = 2
DIFF_Q_ROWS = 512
DIL_Q_ROWS = 128
POST_ROWS = 512
POST_HALVES = 2
FF_CHUNK = 1024

BF16 = jnp.bfloat16
F32 = jnp.float32
NT_DIMS = (((1,), (1,)), ((), ()))


def _rms(x, g, eps):
    return x * lax.rsqrt(jnp.mean(x * x, axis=-1, keepdims=True) + eps) * g


def _proj_kernel(x_ref, g_ref, wqt_ref, wvt_ref, w_ref, cos_ref, sin_ref, cost_ref, sint_ref,
                 qt_ref, kd_ref, vt_ref, dil_ref):
    scale = HEAD_DIM ** -0.5
    half = HEAD_DIM // 2
    sub_rows = x_ref.shape[1] // PROJ_HALVES
    lane = lax.broadcasted_iota(jnp.int32, (sub_rows, LANES), 1)
    first_half = (lane % HEAD_DIM) < half
    slabs = ((kd_ref, 0, True, False), (dil_ref, 0, True, True),
             (dil_ref, D_DIL, True, False), (dil_ref, 2 * D_DIL, False, False))

    for i in range(PROJ_HALVES):
        sl = pl.ds(i * sub_rows, sub_rows)
        hn = _rms(x_ref[0, sl, :], g_ref[...], NORM_EPS).astype(BF16)
        cos = cos_ref[sl, :]
        sin = sin_ref[sl, :]

        def rope(t, cos=cos, sin=sin):
            swapped = jnp.where(first_half,
                                pltpu.roll(t, LANES - half, 1),
                                pltpu.roll(t, half, 1))
            return t * cos + swapped * sin

        qt = lax.dot_general(wqt_ref[...], hn, NT_DIMS, preferred_element_type=F32)
        cos_t = cost_ref[:, sl] * (scale * LOG2_E)
        sin_t = sint_ref[:, sl] * (scale * LOG2_E)
        for hb in range(D_DIFF // HEAD_DIM):
            t1 = qt[hb * HEAD_DIM:hb * HEAD_DIM + half]
            t2 = qt[hb * HEAD_DIM + half:(hb + 1) * HEAD_DIM]
            qt_ref[0, hb * HEAD_DIM:hb * HEAD_DIM + half, sl] = (t1 * cos_t - t2 * sin_t).astype(BF16)
            qt_ref[0, hb * HEAD_DIM + half:(hb + 1) * HEAD_DIM, sl] = (t1 * sin_t + t2 * cos_t).astype(BF16)

        vt_ref[0, :, sl] = lax.dot_general(wvt_ref[...], hn, NT_DIMS,
                                           preferred_element_type=F32).astype(BF16)

        for n, (out_ref, o0, use_rope, use_scale) in enumerate(slabs):
            acc = jnp.dot(hn, w_ref[:, n * D_DIFF:(n + 1) * D_DIFF], preferred_element_type=F32)
            for c in range(D_DIFF // LANES):
                t = acc[:, c * LANES:(c + 1) * LANES]
                if use_rope:
                    t = rope(t)
                if use_scale:
                    t = t * scale
                out_ref[0, sl, o0 + c * LANES:o0 + (c + 1) * LANES] = t.astype(out_ref.dtype)


def _project(x, g_mix, w_in, tables):
    B, S, D = x.shape
    rows = PROJ_ROWS
    cos_t, sin_t, cos_tt, sin_tt = tables
    wqt = w_in[:, :D_DIFF].T.astype(BF16)
    wvt = w_in[:, 2 * D_DIFF:3 * D_DIFF].T.astype(BF16)
    w_rest = jnp.concatenate([w_in[:, D_DIFF:2 * D_DIFF], w_in[:, D_GROUP:]], axis=1).astype(BF16)
    const = lambda a: pl.BlockSpec(a.shape, lambda b, i: (0, 0), pipeline_mode=pl.Buffered(1))
    return pl.pallas_call(
        _proj_kernel,
        grid=(B, S // rows),
        in_specs=[
            pl.BlockSpec((1, rows, D), lambda b, i: (b, i, 0)),
            pl.BlockSpec((1, D), lambda b, i: (0, 0)),
            const(wqt), const(wvt), const(w_rest),
            pl.BlockSpec((rows, LANES), lambda b, i: (i, 0)),
            pl.BlockSpec((rows, LANES), lambda b, i: (i, 0)),
            pl.BlockSpec((HEAD_DIM // 2, rows), lambda b, i: (0, i)),
            pl.BlockSpec((HEAD_DIM // 2, rows), lambda b, i: (0, i)),
        ],
        out_specs=[
            pl.BlockSpec((1, D_DIFF, rows), lambda b, i: (b, 0, i)),
            pl.BlockSpec((1, rows, D_DIFF), lambda b, i: (b, i, 0)),
            pl.BlockSpec((1, D_DIFF, rows), lambda b, i: (b, 0, i)),
            pl.BlockSpec((1, rows, D_GROUP), lambda b, i: (b, i, 0)),
        ],
        out_shape=[
            jax.ShapeDtypeStruct((B, D_DIFF, S), BF16),
            jax.ShapeDtypeStruct((B, S, D_DIFF), BF16),
            jax.ShapeDtypeStruct((B, D_DIFF, S), BF16),
            jax.ShapeDtypeStruct((B, S, D_GROUP), F32),
        ],
        compiler_params=pltpu.CompilerParams(
            dimension_semantics=("parallel", "parallel"),
            vmem_limit_bytes=VMEM_LIMIT_BYTES),
        name="proj_rope",
    )(x, g_mix, wqt, wvt, w_rest, cos_t, sin_t, cos_tt, sin_tt)


def _dil_geometry(window, d, seq):
    radius, L = window // (2 * d), seq // d
    wk = min(DIL_Q_ROWS + 2 * radius, L)
    return radius, L, wk, L // DIL_Q_ROWS


def _dil_bias(seq):
    out = []
    for window, d in DIL_PATTERNS:
        radius, L, wk, blocks = _dil_geometry(window, d, seq)
        row = np.arange(DIL_Q_ROWS)[:, None]
        col = np.arange(wk)[None, :]
        bias = []
        for t in range(d * blocks):
            q0 = (t % blocks) * DIL_Q_ROWS
            k0 = min(max(q0 - radius, 0), L - wk)
            bias.append(np.where(np.abs((k0 + col) - (q0 + row)) <= radius, 0.0, NEG_INF))
        out.append(jnp.asarray(np.stack(bias), dtype=F32))
    return out


def _attn_kernel(qt_ref, k_ref, vt_ref, lq1_ref, lk1_ref, lq2_ref, lk2_ref, g_ref,
                 dq_ref, dk_ref, dv_ref, b0_ref, b1_ref, b2_ref,
                 od_ref, og_ref, s0_scr, s1_scr, n_scr, m_scr, z_scr):
    S = k_ref.shape[1]
    rows = DIFF_Q_ROWS
    nblocks = S // rows
    k = k_ref[0]
    vt_ones = jnp.concatenate([vt_ref[0], jnp.ones((BF16_SUBLANES, S), BF16)], axis=0)
    lam = (jnp.exp(jnp.sum(lq1_ref[...] * lk1_ref[...], axis=-1, keepdims=True))
           - jnp.exp(jnp.sum(lq2_ref[...] * lk2_ref[...], axis=-1, keepdims=True))
           + LAM_INIT)
    gain = g_ref[...] * (1.0 - LAM_INIT)
    feat = lax.broadcasted_iota(jnp.int32, (2 * HEAD_DIM, rows), 0)
    s_scr = (s0_scr, s1_scr)

    def cols(j):
        return pl.ds(j * rows, rows)

    def scores(j, m):
        qt = qt_ref[0, :, cols(j)]
        in_map = (feat >= m * HEAD_DIM) & (feat < (m + 1) * HEAD_DIM)
        qm = jnp.where(in_map, qt, jnp.zeros_like(qt))
        s = jnp.dot(k, qm, preferred_element_type=F32)
        s_scr[m][...] = s
        return jnp.max(s, axis=0, keepdims=True)

    def attend(m, mx):
        p = jnp.exp2(s_scr[m][...] - mx).astype(BF16)
        o = jnp.dot(vt_ones, p, preferred_element_type=F32)
        return o[:2 * HEAD_DIM] / o[2 * HEAD_DIM:2 * HEAD_DIM + 1]

    def finish(j, o0, o1):
        o = o0 - lam * o1
        o = o * lax.rsqrt(jnp.mean(o * o, axis=0, keepdims=True) + SUBLN_EPS) * gain
        od_ref[0, cols(j), :] = o.T.astype(od_ref.dtype)

    qrows = DIL_Q_ROWS
    head0 = lax.broadcasted_iota(jnp.int32, (1, 1, LANES), 2) < HEAD_DIM

    def dilated_scores(j, pi):
        (window, d), bias_ref = DIL_PATTERNS[pi], (b0_ref, b1_ref, b2_ref)[pi]
        radius, L, wk, blocks = _dil_geometry(window, d, S)
        per = d * blocks // nblocks
        qsl, ksl = [], []
        for i in range(per):
            t = j * per + i
            r, q0 = t // blocks, (t % blocks) * qrows
            k0 = min(max(q0 - radius, 0), L - wk)
            qsl.append(pl.ds(q0 * d + r, qrows, stride=d))
            ksl.append(pl.ds(k0 * d + r, wk, stride=d))
        qt = jnp.stack([dq_ref[0, sl, :].astype(BF16) for sl in qsl])
        kt = jnp.stack([dk_ref[0, sl, :].astype(BF16) for sl in ksl])
        vt = jnp.stack([dv_ref[0, sl, :].astype(BF16) for sl in ksl])
        bias = bias_ref[pl.ds(j * per, per)]

        def per_head(x):
            zero = jnp.zeros_like(x)
            return jnp.concatenate([jnp.where(head0, x, zero), jnp.where(head0, zero, x)], axis=1)

        s = jnp.einsum("tqc,tkc->tqk", qt, per_head(kt), preferred_element_type=F32)
        ps, mxs, zs = [], [], []
        for h in range(2):
            sh = s[:, :, h * wk:(h + 1) * wk] + bias
            mx = jnp.max(sh, axis=-1, keepdims=True)
            p = jnp.exp(sh - mx)
            ps.append(p.astype(BF16))
            mxs.append(mx)
            zs.append(jnp.sum(p, axis=-1, keepdims=True))
        mx = jnp.where(head0, mxs[0], mxs[1])
        z = jnp.where(head0, zs[0], zs[1])
        return pi, qsl, per_head(vt), jnp.concatenate(ps, axis=-1), mx, z

    def dilated_values(ctx):
        pi, qsl, v2, p, mx, z = ctx
        n = jnp.einsum("tqk,tkc->tqc", p, v2, preferred_element_type=F32)
        for i, dst in enumerate(qsl):
            n_scr[pi, dst, :] = n[i]
            m_scr[pi, dst, :] = mx[i]
            z_scr[pi, dst, :] = z[i]

    groups = iter([(j, pi) for j in range(nblocks) for pi in range(len(DIL_PATTERNS))])

    def hosted(long_matmul):
        ctx = dilated_scores(*next(groups))
        out = long_matmul()
        dilated_values(ctx)
        return out

    mx0 = hosted(lambda: scores(0, 0))
    for j in range(nblocks):
        mx1 = hosted(lambda: scores(j, 1))
        o0 = hosted(lambda: attend(0, mx0))
        if j + 1 < nblocks:
            mx0 = hosted(lambda: scores(j + 1, 0))
        o1 = attend(1, mx1)
        finish(j, o0, o1)

    chunk = 256

    def combine(c, carry):
        sl = pl.ds(pl.multiple_of(c * chunk, chunk), chunk)
        ms = [m_scr[pi, sl, :] for pi in range(len(DIL_PATTERNS))]
        top = functools.reduce(jnp.maximum, ms)
        num = jnp.zeros((chunk, LANES), F32)
        den = jnp.zeros((chunk, LANES), F32)
        for pi in range(len(DIL_PATTERNS)):
            a = jnp.exp(ms[pi] - top)
            num = num + a * n_scr[pi, sl, :]
            den = den + a * z_scr[pi, sl, :]
        og_ref[0, sl, :] = (num / den).astype(og_ref.dtype)
        return carry

    lax.fori_loop(0, S // chunk, combine, 0)


def _attention(qt, kd, vt, dil_qkv, lq1, lk1, lq2, lk2, g_subln):
    B, S, _ = kd.shape
    nblk = D_DIL // LANES
    assert nblk == N_HEADS_DIFF
    small = lambda n: pl.BlockSpec((1, n), lambda b, h: (0, 0))
    biases = _dil_bias(S)
    const = lambda a: pl.BlockSpec(a.shape, lambda b, h: (0, 0, 0), pipeline_mode=pl.Buffered(1))
    s_scr = pltpu.VMEM((S, DIFF_Q_ROWS), F32)
    stat_scr = pltpu.VMEM((len(DIL_PATTERNS), S, LANES), F32)
    return pl.pallas_call(
        _attn_kernel,
        grid=(B, N_HEADS_DIFF),
        in_specs=[
            pl.BlockSpec((1, LANES, S), lambda b, h: (b, h, 0)),
            pl.BlockSpec((1, S, LANES), lambda b, h: (b, 0, h)),
            pl.BlockSpec((1, LANES, S), lambda b, h: (b, h, 0)),
            small(HEAD_DIM), small(HEAD_DIM), small(HEAD_DIM), small(HEAD_DIM),
            pl.BlockSpec((2 * HEAD_DIM, 1), lambda b, h: (0, 0)),
            pl.BlockSpec((1, S, LANES), lambda b, h: (b, 0, h)),
            pl.BlockSpec((1, S, LANES), lambda b, h: (b, 0, nblk + h)),
            pl.BlockSpec((1, S, LANES), lambda b, h: (b, 0, 2 * nblk + h)),
        ] + [const(a) for a in biases],
        out_specs=[
            pl.BlockSpec((1, S, LANES), lambda b, h: (b, 0, h)),
            pl.BlockSpec((1, S, LANES), lambda b, h: (b, 0, h)),
        ],
        out_shape=[
            jax.ShapeDtypeStruct((B, S, D_DIFF), BF16),
            jax.ShapeDtypeStruct((B, S, D_DIL), BF16),
        ],
        scratch_shapes=[s_scr, s_scr, stat_scr, stat_scr, stat_scr],
        compiler_params=pltpu.CompilerParams(
            dimension_semantics=("parallel", "parallel"),
            vmem_limit_bytes=VMEM_LIMIT_BYTES),
        name="attention",
    )(qt, kd, vt, lq1, lk1, lq2, lk2, g_subln.reshape(2 * HEAD_DIM, 1),
      dil_qkv, dil_qkv, dil_qkv, *biases)


def _post_kernel(x_ref, od_ref, og_ref, p_ref, wo_ref, gm_ref, wu_ref, wd_ref,
                 gp_ref, wg_ref, wp_ref, gf_ref, o_ref):
    half_rows = x_ref.shape[0] // POST_HALVES
    halves = [pl.ds(i * half_rows, half_rows) for i in range(POST_HALVES)]

    h = []
    for sl in halves:
        mixed = jnp.concatenate([od_ref[sl, :], og_ref[sl, :]], axis=-1)
        h.append(x_ref[sl, :] + jnp.dot(mixed, wo_ref[...], preferred_element_type=F32))

    for i in range(POST_HALVES):
        hn = _rms(h[i], gm_ref[...], NORM_EPS).astype(BF16)
        for c in range(D_FF // FF_CHUNK):
            u = jnp.dot(hn, wu_ref[:, c * FF_CHUNK:(c + 1) * FF_CHUNK], preferred_element_type=F32)
            u = jnp.square(jnp.maximum(u, 0.0)).astype(BF16)
            h[i] = h[i] + jnp.dot(u, wd_ref[c * FF_CHUNK:(c + 1) * FF_CHUNK, :],
                                  preferred_element_type=F32)

    for i, sl in enumerate(halves):
        hn = _rms(h[i], gp_ref[...], NORM_EPS).astype(BF16)
        gate = jax.nn.sigmoid(jnp.dot(hn, wg_ref[...], preferred_element_type=F32))
        emb = jnp.dot(p_ref[sl, :].astype(BF16), wp_ref[...], preferred_element_type=F32)
        o_ref[sl, :] = _rms(h[i] + gate * emb, gf_ref[...], NORM_EPS)


def _post(x2, od2, og2, p2, w_out, g_mlp, w_up, w_down, g_ple, w_gate, w_ple, g_final):
    N, D = x2.shape
    rows = POST_ROWS
    tok = lambda width: pl.BlockSpec((rows, width), lambda i: (i, 0))
    whole = lambda a: pl.BlockSpec(a.shape, lambda i: (0, 0), pipeline_mode=pl.Buffered(1))
    return pl.pallas_call(
        _post_kernel,
        grid=(N // rows,),
        in_specs=[
            tok(D), tok(D_DIFF), tok(D_DIL), tok(D_PLE),
            whole(w_out), whole(g_mlp), whole(w_up), whole(w_down),
            whole(g_ple), whole(w_gate), whole(w_ple), whole(g_final),
        ],
        out_specs=tok(D),
        out_shape=jax.ShapeDtypeStruct((N, D), F32),
        compiler_params=pltpu.CompilerParams(
            dimension_semantics=("parallel",),
            vmem_limit_bytes=VMEM_LIMIT_BYTES),
        name="out_mlp_ple",
    )(x2, od2, og2, p2, w_out, g_mlp, w_up, w_down, g_ple, w_gate, w_ple, g_final)


def _rope_tables(seq):
    inv = ROPE_THETA ** (-jnp.arange(0, HEAD_DIM, 2, dtype=F32) / HEAD_DIM)
    ang = jnp.arange(seq, dtype=F32)[:, None] * inv[None, :]
    cos, sin = jnp.cos(ang), jnp.sin(ang)
    reps = LANES // HEAD_DIM
    cos_t = jnp.tile(jnp.concatenate([cos, cos], axis=-1), (1, reps))
    sin_t = jnp.tile(jnp.concatenate([-sin, sin], axis=-1), (1, reps))
    return cos_t, sin_t, cos.T, sin.T


def kernel(x, p, w_in, w_out, g_mix, lambda_q1, lambda_k1, lambda_q2, lambda_k2, g_subln,
           g_mlp, w_up, w_down, g_ple, w_ple_gate, w_ple_proj, g_final):
    B, S, D = x.shape
    assert x.shape[1:] == (2048, D_MODEL) and w_in.shape == (1, D_MODEL, 2 * D_GROUP)

    qt, kd, vt, dil_qkv = _project(x, g_mix, w_in[0], _rope_tables(S))
    od, og = _attention(qt, kd, vt, dil_qkv, lambda_q1, lambda_k1, lambda_q2, lambda_k2, g_subln)

    out = _post(
        x.reshape(B * S, D), od.reshape(B * S, D_DIFF), og.reshape(B * S, D_DIL),
        p[0].reshape(B * S, D_PLE),
        w_out[0].astype(BF16), g_mlp, w_up[0].astype(BF16), w_down[0].astype(BF16),
        g_ple, w_ple_gate[0].astype(BF16), w_ple_proj[0].astype(BF16), g_final.reshape(1, D))
    return out.reshape(B, S, D)
```

```python
import functools
import math

import jax
import jax.numpy as jnp
import numpy as np
from jax import lax
from jax.experimental import pallas as pl
from jax.experimental.pallas import tpu as pltpu

D_MODEL = 1024
HEAD_DIM = 64
D_DIFF = 512
D_DIL = 512
D_GROUP = 3 * D_DIFF
N_HEADS_DIFF = 4
DIL_PATTERNS = ((128, 1), (512, 4), (2048, 16))
D_FF = 4 * D_MODEL
D_PLE = 256
ROPE_THETA = 10000.0
NORM_EPS = 1e-6
SUBLN_EPS = 1e-5
NEG_INF = -1e30
LAM_INIT = 0.8 - 0.6 * math.exp(-0.3 * 0)
LOG2_E = math.log2(math.e)

LANES = 128
BF16_SUBLANES = 16
VMEM_LIMIT_BYTES = 56 * 1024 * 1024

PROJ_ROWS = 1024
PROJ_HALVES = 2
DIFF_Q_ROWS = 512
DIL_Q_ROWS = 128
POST_ROWS = 512
POST_HALVES = 2
FF_CHUNK = 1024

BF16 = jnp.bfloat16
F32 = jnp.float32
NT_DIMS = (((1,), (1,)), ((), ()))


def _rms(x, g, eps):
    return x * lax.rsqrt(jnp.mean(x * x, axis=-1, keepdims=True) + eps) * g


def _proj_kernel(x_ref, g_ref, wqt_ref, wvt_ref, w_ref, cos_ref, sin_ref, cost_ref, sint_ref,
                 qt_ref, kd_ref, vt_ref, dil_ref):
    scale = HEAD_DIM ** -0.5 * LOG2_E
    half = HEAD_DIM // 2
    sub_rows = x_ref.shape[1] // PROJ_HALVES
    lane = lax.broadcasted_iota(jnp.int32, (sub_rows, LANES), 1)
    first_half = (lane % HEAD_DIM) < half
    slabs = ((kd_ref, 0, True, False), (dil_ref, 0, True, True),
             (dil_ref, D_DIL, True, False), (dil_ref, 2 * D_DIL, False, False))

    for i in range(PROJ_HALVES):
        sl = pl.ds(i * sub_rows, sub_rows)
        hn = _rms(x_ref[0, sl, :], g_ref[...], NORM_EPS).astype(BF16)
        cos = cos_ref[sl, :]
        sin = sin_ref[sl, :]

        def rope(t, cos=cos, sin=sin):
            swapped = jnp.where(first_half,
                                pltpu.roll(t, LANES - half, 1),
                                pltpu.roll(t, half, 1))
            return t * cos + swapped * sin

        qt = lax.dot_general(wqt_ref[...], hn, NT_DIMS, preferred_element_type=F32)
        cos_t = cost_ref[:, sl] * scale
        sin_t = sint_ref[:, sl] * scale
        for hb in range(D_DIFF // HEAD_DIM):
            t1 = qt[hb * HEAD_DIM:hb * HEAD_DIM + half]
            t2 = qt[hb * HEAD_DIM + half:(hb + 1) * HEAD_DIM]
            qt_ref[0, hb * HEAD_DIM:hb * HEAD_DIM + half, sl] = (t1 * cos_t - t2 * sin_t).astype(BF16)
            qt_ref[0, hb * HEAD_DIM + half:(hb + 1) * HEAD_DIM, sl] = (t1 * sin_t + t2 * cos_t).astype(BF16)

        vt_ref[0, :, sl] = lax.dot_general(wvt_ref[...], hn, NT_DIMS,
                                           preferred_element_type=F32).astype(BF16)

        for n, (out_ref, o0, use_rope, use_scale) in enumerate(slabs):
            acc = jnp.dot(hn, w_ref[:, n * D_DIFF:(n + 1) * D_DIFF], preferred_element_type=F32)
            for c in range(D_DIFF // LANES):
                t = acc[:, c * LANES:(c + 1) * LANES]
                if use_rope:
                    t = rope(t)
                if use_scale:
                    t = t * scale
                out_ref[0, sl, o0 + c * LANES:o0 + (c + 1) * LANES] = t.astype(out_ref.dtype)


def _project(x, g_mix, w_in, tables):
    B, S, D = x.shape
    rows = PROJ_ROWS
    cos_t, sin_t, cos_tt, sin_tt = tables
    wqt = w_in[:, :D_DIFF].T.astype(BF16)
    wvt = w_in[:, 2 * D_DIFF:3 * D_DIFF].T.astype(BF16)
    w_rest = jnp.concatenate([w_in[:, D_DIFF:2 * D_DIFF], w_in[:, D_GROUP:]], axis=1).astype(BF16)
    const = lambda a: pl.BlockSpec(a.shape, lambda b, i: (0, 0), pipeline_mode=pl.Buffered(1))
    return pl.pallas_call(
        _proj_kernel,
        grid=(B, S // rows),
        in_specs=[
            pl.BlockSpec((1, rows, D), lambda b, i: (b, i, 0)),
            pl.BlockSpec((1, D), lambda b, i: (0, 0)),
            const(wqt), const(wvt), const(w_rest),
            pl.BlockSpec((rows, LANES), lambda b, i: (i, 0)),
            pl.BlockSpec((rows, LANES), lambda b, i: (i, 0)),
            pl.BlockSpec((HEAD_DIM // 2, rows), lambda b, i: (0, i)),
            pl.BlockSpec((HEAD_DIM // 2, rows), lambda b, i: (0, i)),
        ],
        out_specs=[
            pl.BlockSpec((1, D_DIFF, rows), lambda b, i: (b, 0, i)),
            pl.BlockSpec((1, rows, D_DIFF), lambda b, i: (b, i, 0)),
            pl.BlockSpec((1, D_DIFF, rows), lambda b, i: (b, 0, i)),
            pl.BlockSpec((1, rows, D_GROUP), lambda b, i: (b, i, 0)),
        ],
        out_shape=[
            jax.ShapeDtypeStruct((B, D_DIFF, S), BF16),
            jax.ShapeDtypeStruct((B, S, D_DIFF), BF16),
            jax.ShapeDtypeStruct((B, D_DIFF, S), BF16),
            jax.ShapeDtypeStruct((B, S, D_GROUP), F32),
        ],
        compiler_params=pltpu.CompilerParams(
            dimension_semantics=("parallel", "parallel"),
            vmem_limit_bytes=VMEM_LIMIT_BYTES),
        name="proj_rope",
    )(x, g_mix, wqt, wvt, w_rest, cos_t, sin_t, cos_tt, sin_tt)


def _dil_geometry(window, d, seq):
    radius, L = window // (2 * d), seq // d
    wk = min(DIL_Q_ROWS + 2 * radius, L)
    return radius, L, wk, L // DIL_Q_ROWS


def _dil_bias(seq):
    out = []
    for window, d in DIL_PATTERNS:
        radius, L, wk, blocks = _dil_geometry(window, d, seq)
        row = np.arange(DIL_Q_ROWS)[:, None]
        col = np.arange(wk)[None, :]
        bias = []
        for t in range(d * blocks):
            q0 = (t % blocks) * DIL_Q_ROWS
            k0 = min(max(q0 - radius, 0), L - wk)
            bias.append(np.where(np.abs((k0 + col) - (q0 + row)) <= radius, 0.0, NEG_INF))
        out.append(jnp.asarray(np.stack(bias), dtype=F32))
    return out


def _attn_kernel(qt_ref, k_ref, vt_ref, lq1_ref, lk1_ref, lq2_ref, lk2_ref, g_ref,
                 dq_ref, dk_ref, dv_ref, b0_ref, b1_ref, b2_ref,
                 od_ref, og_ref, s0_scr, s1_scr, n_scr, m_scr, z_scr):
    S = k_ref.shape[1]
    rows = DIFF_Q_ROWS
    nblocks = S // rows
    k = k_ref[0]
    vt_ones = jnp.concatenate([vt_ref[0], jnp.ones((BF16_SUBLANES, S), BF16)], axis=0)
    lam = (jnp.exp(jnp.sum(lq1_ref[...] * lk1_ref[...], axis=-1, keepdims=True))
           - jnp.exp(jnp.sum(lq2_ref[...] * lk2_ref[...], axis=-1, keepdims=True))
           + LAM_INIT)
    gain = g_ref[...] * (1.0 - LAM_INIT)
    feat = lax.broadcasted_iota(jnp.int32, (2 * HEAD_DIM, rows), 0)
    s_scr = (s0_scr, s1_scr)

    def cols(j):
        return pl.ds(j * rows, rows)

    def scores(j, m):
        qt = qt_ref[0, :, cols(j)]
        in_map = (feat >= m * HEAD_DIM) & (feat < (m + 1) * HEAD_DIM)
        qm = jnp.where(in_map, qt, jnp.zeros_like(qt))
        s = jnp.dot(k, qm, preferred_element_type=F32)
        s_scr[m][...] = s
        return jnp.max(s, axis=0, keepdims=True)

    def numerators(m, mx):
        return jnp.exp2(s_scr[m][...] - mx).astype(BF16)

    def weighted_values(p):
        o = jnp.dot(vt_ones, p, preferred_element_type=F32)
        return o[:2 * HEAD_DIM] / o[2 * HEAD_DIM:2 * HEAD_DIM + 1]

    def finish(j, o0, o1):
        o = o0 - lam * o1
        o = o * lax.rsqrt(jnp.mean(o * o, axis=0, keepdims=True) + SUBLN_EPS) * gain
        od_ref[0, cols(j), :] = o.T.astype(od_ref.dtype)

    qrows = DIL_Q_ROWS
    head0 = lax.broadcasted_iota(jnp.int32, (1, 1, LANES), 2) < HEAD_DIM

    def dilated_scores(j, pi):
        (window, d), bias_ref = DIL_PATTERNS[pi], (b0_ref, b1_ref, b2_ref)[pi]
        radius, L, wk, blocks = _dil_geometry(window, d, S)
        per = d * blocks // nblocks
        qsl, ksl = [], []
        for i in range(per):
            t = j * per + i
            r, q0 = t // blocks, (t % blocks) * qrows
            k0 = min(max(q0 - radius, 0), L - wk)
            qsl.append(pl.ds(q0 * d + r, qrows, stride=d))
            ksl.append(pl.ds(k0 * d + r, wk, stride=d))
        qt = jnp.stack([dq_ref[0, sl, :].astype(BF16) for sl in qsl])
        kt = jnp.stack([dk_ref[0, sl, :].astype(BF16) for sl in ksl])
        vt = jnp.stack([dv_ref[0, sl, :].astype(BF16) for sl in ksl])
        bias = bias_ref[pl.ds(j * per, per)]

        def per_head(x):
            zero = jnp.zeros_like(x)
            return jnp.concatenate([jnp.where(head0, x, zero), jnp.where(head0, zero, x)], axis=1)

        s = jnp.einsum("tqc,tkc->tqk", qt, per_head(kt), preferred_element_type=F32)
        ps, mxs, zs = [], [], []
        for h in range(2):
            sh = s[:, :, h * wk:(h + 1) * wk] + bias
            mx = jnp.max(sh, axis=-1, keepdims=True)
            p = jnp.exp2(sh - mx)
            ps.append(p.astype(BF16))
            mxs.append(mx)
            zs.append(jnp.sum(p, axis=-1, keepdims=True))
        mx = jnp.where(head0, mxs[0], mxs[1])
        z = jnp.where(head0, zs[0], zs[1])
        return pi, qsl, per_head(vt), jnp.concatenate(ps, axis=-1), mx, z

    def dilated_values(ctx):
        pi, qsl, v2, p, mx, z = ctx
        n = jnp.einsum("tqk,tkc->tqc", p, v2, preferred_element_type=F32)
        for i, dst in enumerate(qsl):
            n_scr[pi, dst, :] = n[i]
            m_scr[pi, dst, :] = mx[i]
            z_scr[pi, dst, :] = z[i]

    groups = iter([(j, pi) for j in range(nblocks) for pi in range(len(DIL_PATTERNS))])

    def hosted(long_matmul):
        ctx = dilated_scores(*next(groups))
        out = long_matmul()
        dilated_values(ctx)
        return out

    mx0 = hosted(lambda: scores(0, 0))
    for j in range(nblocks):
        p0 = numerators(0, mx0)
        mx1 = hosted(lambda: scores(j, 1))
        o0 = weighted_values(p0)
        p1 = numerators(1, mx1)
        if j + 1 < nblocks:
            mx0 = hosted(lambda: scores(j + 1, 0))
        o1 = hosted(lambda: weighted_values(p1))
        finish(j, o0, o1)

    chunk = 256

    def combine(c, carry):
        sl = pl.ds(pl.multiple_of(c * chunk, chunk), chunk)
        ms = [m_scr[pi, sl, :] for pi in range(len(DIL_PATTERNS))]
        top = functools.reduce(jnp.maximum, ms)
        num = jnp.zeros((chunk, LANES), F32)
        den = jnp.zeros((chunk, LANES), F32)
        for pi in range(len(DIL_PATTERNS)):
            a = jnp.exp2(ms[pi] - top)
            num = num + a * n_scr[pi, sl, :]
            den = den + a * z_scr[pi, sl, :]
        og_ref[0, sl, :] = (num / den).astype(og_ref.dtype)
        return carry

    lax.fori_loop(0, S // chunk, combine, 0)


def _attention(qt, kd, vt, dil_qkv, lq1, lk1, lq2, lk2, g_subln):
    B, S, _ = kd.shape
    nblk = D_DIL // LANES
    assert nblk == N_HEADS_DIFF
    small = lambda n: pl.BlockSpec((1, n), lambda b, h: (0, 0))
    biases = _dil_bias(S)
    const = lambda a: pl.BlockSpec(a.shape, lambda b, h: (0, 0, 0), pipeline_mode=pl.Buffered(1))
    s_scr = pltpu.VMEM((S, DIFF_Q_ROWS), F32)
    stat_scr = pltpu.VMEM((len(DIL_PATTERNS), S, LANES), F32)
    return pl.pallas_call(
        _attn_kernel,
        grid=(B, N_HEADS_DIFF),
        in_specs=[
            pl.BlockSpec((1, LANES, S), lambda b, h: (b, h, 0)),
            pl.BlockSpec((1, S, LANES), lambda b, h: (b, 0, h)),
            pl.BlockSpec((1, LANES, S), lambda b, h: (b, h, 0)),
            small(HEAD_DIM), small(HEAD_DIM), small(HEAD_DIM), small(HEAD_DIM),
            pl.BlockSpec((2 * HEAD_DIM, 1), lambda b, h: (0, 0)),
            pl.BlockSpec((1, S, LANES), lambda b, h: (b, 0, h)),
            pl.BlockSpec((1, S, LANES), lambda b, h: (b, 0, nblk + h)),
            pl.BlockSpec((1, S, LANES), lambda b, h: (b, 0, 2 * nblk + h)),
        ] + [const(a) for a in biases],
        out_specs=[
            pl.BlockSpec((1, S, LANES), lambda b, h: (b, 0, h)),
            pl.BlockSpec((1, S, LANES), lambda b, h: (b, 0, h)),
        ],
        out_shape=[
            jax.ShapeDtypeStruct((B, S, D_DIFF), BF16),
            jax.ShapeDtypeStruct((B, S, D_DIL), BF16),
        ],
        scratch_shapes=[s_scr, s_scr, stat_scr, stat_scr, stat_scr],
        compiler_params=pltpu.CompilerParams(
            dimension_semantics=("parallel", "parallel"),
            vmem_limit_bytes=VMEM_LIMIT_BYTES),
        name="attention",
    )(qt, kd, vt, lq1, lk1, lq2, lk2, g_subln.reshape(2 * HEAD_DIM, 1),
      dil_qkv, dil_qkv, dil_qkv, *biases)


def _post_kernel(x_ref, od_ref, og_ref, p_ref, wo_ref, gm_ref, wu_ref, wd_ref,
                 gp_ref, wg_ref, wp_ref, gf_ref, o_ref):
    half_rows = x_ref.shape[0] // POST_HALVES
    halves = [pl.ds(i * half_rows, half_rows) for i in range(POST_HALVES)]

    h = []
    for sl in halves:
        mixed = jnp.concatenate([od_ref[sl, :], og_ref[sl, :]], axis=-1)
        h.append(x_ref[sl, :] + jnp.dot(mixed, wo_ref[...], preferred_element_type=F32))

    for i in range(POST_HALVES):
        hn = _rms(h[i], gm_ref[...], NORM_EPS).astype(BF16)
        for c in range(D_FF // FF_CHUNK):
            u = jnp.dot(hn, wu_ref[:, c * FF_CHUNK:(c + 1) * FF_CHUNK], preferred_element_type=F32)
            u = jnp.square(jnp.maximum(u, 0.0)).astype(BF16)
            h[i] = h[i] + jnp.dot(u, wd_ref[c * FF_CHUNK:(c + 1) * FF_CHUNK, :],
                                  preferred_element_type=F32)

    for i, sl in enumerate(halves):
        hn = _rms(h[i], gp_ref[...], NORM_EPS).astype(BF16)
        gate = jax.nn.sigmoid(jnp.dot(hn, wg_ref[...], preferred_element_type=F32))
        emb = jnp.dot(p_ref[sl, :].astype(BF16), wp_ref[...], preferred_element_type=F32)
        o_ref[sl, :] = _rms(h[i] + gate * emb, gf_ref[...], NORM_EPS)


def _post(x2, od2, og2, p2, w_out, g_mlp, w_up, w_down, g_ple, w_gate, w_ple, g_final):
    N, D = x2.shape
    rows = POST_ROWS
    tok = lambda width: pl.BlockSpec((rows, width), lambda i: (i, 0))
    whole = lambda a: pl.BlockSpec(a.shape, lambda i: (0, 0), pipeline_mode=pl.Buffered(1))
    return pl.pallas_call(
        _post_kernel,
        grid=(N // rows,),
        in_specs=[
            tok(D), tok(D_DIFF), tok(D_DIL), tok(D_PLE),
            whole(w_out), whole(g_mlp), whole(w_up), whole(w_down),
            whole(g_ple), whole(w_gate), whole(w_ple), whole(g_final),
        ],
        out_specs=tok(D),
        out_shape=jax.ShapeDtypeStruct((N, D), F32),
        compiler_params=pltpu.CompilerParams(
            dimension_semantics=("parallel",),
            vmem_limit_bytes=VMEM_LIMIT_BYTES),
        name="out_mlp_ple",
    )(x2, od2, og2, p2, w_out, g_mlp, w_up, w_down, g_ple, w_gate, w_ple, g_final)


def _rope_tables(seq):
    inv = ROPE_THETA ** (-jnp.arange(0, HEAD_DIM, 2, dtype=F32) / HEAD_DIM)
    ang = jnp.arange(seq, dtype=F32)[:, None] * inv[None, :]
    cos, sin = jnp.cos(ang), jnp.sin(ang)
    reps = LANES // HEAD_DIM
    cos_t = jnp.tile(jnp.concatenate([cos, cos], axis=-1), (1, reps))
    sin_t = jnp.tile(jnp.concatenate([-sin, sin], axis=-1), (1, reps))
    return cos_t, sin_t, cos.T, sin.T


def kernel(x, p, w_in, w_out, g_mix, lambda_q1, lambda_k1, lambda_q2, lambda_k2, g_subln,
           g_mlp, w_up, w_down, g_ple, w_ple_gate, w_ple_proj, g_final):
    B, S, D = x.shape
    assert x.shape[1:] == (2048, D_MODEL) and w_in.shape == (1, D_MODEL, 2 * D_GROUP)

    qt, kd, vt, dil_qkv = _project(x, g_mix, w_in[0], _rope_tables(S))
    od, og = _attention(qt, kd, vt, dil_qkv, lambda_q1, lambda_k1, lambda_q2, lambda_k2, g_subln)

    out = _post(
        x.reshape(B * S, D), od.reshape(B * S, D_DIFF), og.reshape(B * S, D_DIL),
        p[0].reshape(B * S, D_PLE),
        w_out[0].astype(BF16), g_mlp, w_up[0].astype(BF16), w_down[0].astype(BF16),
        g_ple, w_ple_gate[0].astype(BF16), w_ple_proj[0].astype(BF16), g_final.reshape(1, D))
    return out.reshape(B, S, D)
```

```python
import functools
import math

import jax
import jax.numpy as jnp
import numpy as np
from jax import lax
from jax.experimental import pallas as pl
from jax.experimental.pallas import tpu as pltpu

D_MODEL = 1024
HEAD_DIM = 64
D_DIFF = 512
D_DIL = 512
D_GROUP = 3 * D_DIFF
N_HEADS_DIFF = 4
DIL_PATTERNS = ((128, 1), (512, 4), (2048, 16))
D_FF = 4 * D_MODEL
D_PLE = 256
ROPE_THETA = 10000.0
NORM_EPS = 1e-6
SUBLN_EPS = 1e-5
NEG_INF = -1e30
LAM_INIT = 0.8 - 0.6 * math.exp(-0.3 * 0)
LOG2_E = math.log2(math.e)

LANES = 128
BF16_SUBLANES = 16
VMEM_LIMIT_BYTES = 56 * 1024 * 1024

PROJ_ROWS = 1024
PROJ_HALVES = 2
DIFF_Q_ROWS = 512
DIL_Q_ROWS = 128
POST_ROWS = 512
POST_HALVES = 2
FF_CHUNK = 1024

BF16 = jnp.bfloat16
F32 = jnp.float32
NT_DIMS = (((1,), (1,)), ((), ()))


def _rms(x, g, eps):
    return x * lax.rsqrt(jnp.mean(x * x, axis=-1, keepdims=True) + eps) * g


def _proj_kernel(x_ref, g_ref, wqt_ref, wvt_ref, w_ref, cos_ref, sin_ref, cost_ref, sint_ref,
                 qt_ref, kd_ref, vt_ref, dil_ref):
    scale = HEAD_DIM ** -0.5 * LOG2_E
    half = HEAD_DIM // 2
    sub_rows = x_ref.shape[1] // PROJ_HALVES
    lane = lax.broadcasted_iota(jnp.int32, (sub_rows, LANES), 1)
    first_half = (lane % HEAD_DIM) < half
    slabs = ((kd_ref, 0, True, False), (dil_ref, 0, True, True),
             (dil_ref, D_DIL, True, False), (dil_ref, 2 * D_DIL, False, False))

    for i in range(PROJ_HALVES):
        sl = pl.ds(i * sub_rows, sub_rows)
        hn = _rms(x_ref[0, sl, :], g_ref[...], NORM_EPS).astype(BF16)
        cos = cos_ref[sl, :]
        sin = sin_ref[sl, :]

        def rope(t, cos=cos, sin=sin):
            swapped = jnp.where(first_half,
                                pltpu.roll(t, LANES - half, 1),
                                pltpu.roll(t, half, 1))
            return t * cos + swapped * sin

        qt = lax.dot_general(wqt_ref[...], hn, NT_DIMS, preferred_element_type=F32)
        cos_t = cost_ref[:, sl] * scale
        sin_t = sint_ref[:, sl] * scale
        for hb in range(D_DIFF // HEAD_DIM):
            t1 = qt[hb * HEAD_DIM:hb * HEAD_DIM + half]
            t2 = qt[hb * HEAD_DIM + half:(hb + 1) * HEAD_DIM]
            qt_ref[0, hb * HEAD_DIM:hb * HEAD_DIM + half, sl] = (t1 * cos_t - t2 * sin_t).astype(BF16)
            qt_ref[0, hb * HEAD_DIM + half:(hb + 1) * HEAD_DIM, sl] = (t1 * sin_t + t2 * cos_t).astype(BF16)

        vt_ref[0, :, sl] = lax.dot_general(wvt_ref[...], hn, NT_DIMS,
                                           preferred_element_type=F32).astype(BF16)

        for n, (out_ref, o0, use_rope, use_scale) in enumerate(slabs):
            acc = jnp.dot(hn, w_ref[:, n * D_DIFF:(n + 1) * D_DIFF], preferred_element_type=F32)
            for c in range(D_DIFF // LANES):
                t = acc[:, c * LANES:(c + 1) * LANES]
                if use_rope:
                    t = rope(t)
                if use_scale:
                    t = t * scale
                out_ref[0, sl, o0 + c * LANES:o0 + (c + 1) * LANES] = t.astype(out_ref.dtype)


def _project(x, g_mix, w_in, tables):
    B, S, D = x.shape
    rows = PROJ_ROWS
    cos_t, sin_t, cos_tt, sin_tt = tables
    wqt = w_in[:, :D_DIFF].T.astype(BF16)
    wvt = w_in[:, 2 * D_DIFF:3 * D_DIFF].T.astype(BF16)
    w_rest = jnp.concatenate([w_in[:, D_DIFF:2 * D_DIFF], w_in[:, D_GROUP:]], axis=1).astype(BF16)
    const = lambda a: pl.BlockSpec(a.shape, lambda b, i: (0, 0), pipeline_mode=pl.Buffered(1))
    return pl.pallas_call(
        _proj_kernel,
        grid=(B, S // rows),
        in_specs=[
            pl.BlockSpec((1, rows, D), lambda b, i: (b, i, 0)),
            pl.BlockSpec((1, D), lambda b, i: (0, 0)),
            const(wqt), const(wvt), const(w_rest),
            pl.BlockSpec((rows, LANES), lambda b, i: (i, 0)),
            pl.BlockSpec((rows, LANES), lambda b, i: (i, 0)),
            pl.BlockSpec((HEAD_DIM // 2, rows), lambda b, i: (0, i)),
            pl.BlockSpec((HEAD_DIM // 2, rows), lambda b, i: (0, i)),
        ],
        out_specs=[
            pl.BlockSpec((1, D_DIFF, rows), lambda b, i: (b, 0, i)),
            pl.BlockSpec((1, rows, D_DIFF), lambda b, i: (b, i, 0)),
            pl.BlockSpec((1, D_DIFF, rows), lambda b, i: (b, 0, i)),
            pl.BlockSpec((1, rows, D_GROUP), lambda b, i: (b, i, 0)),
        ],
        out_shape=[
            jax.ShapeDtypeStruct((B, D_DIFF, S), BF16),
            jax.ShapeDtypeStruct((B, S, D_DIFF), BF16),
            jax.ShapeDtypeStruct((B, D_DIFF, S), BF16),
            jax.ShapeDtypeStruct((B, S, D_GROUP), F32),
        ],
        compiler_params=pltpu.CompilerParams(
            dimension_semantics=("parallel", "parallel"),
            vmem_limit_bytes=VMEM_LIMIT_BYTES),
        name="proj_rope",
    )(x, g_mix, wqt, wvt, w_rest, cos_t, sin_t, cos_tt, sin_tt)


def _dil_geometry(window, d, seq):
    radius, L = window // (2 * d), seq // d
    wk = min(DIL_Q_ROWS + 2 * radius, L)
    return radius, L, wk, L // DIL_Q_ROWS


def _dil_bias(seq):
    out = []
    for window, d in DIL_PATTERNS:
        radius, L, wk, blocks = _dil_geometry(window, d, seq)
        row = np.arange(DIL_Q_ROWS)[:, None]
        col = np.arange(wk)[None, :]
        bias = []
        for t in range(d * blocks):
            q0 = (t % blocks) * DIL_Q_ROWS
            k0 = min(max(q0 - radius, 0), L - wk)
            bias.append(np.where(np.abs((k0 + col) - (q0 + row)) <= radius, 0.0, NEG_INF))
        out.append(jnp.asarray(np.stack(bias), dtype=F32))
    return out


def _attn_kernel(qt_ref, k_ref, vt_ref, lq1_ref, lk1_ref, lq2_ref, lk2_ref, g_ref,
                 dq_ref, dk_ref, dv_ref, b0_ref, b1_ref, b2_ref,
                 od_ref, og_ref, s0_scr, s1_scr, n_scr, m_scr, z_scr):
    S = k_ref.shape[1]
    rows = DIFF_Q_ROWS
    nblocks = S // rows
    k = k_ref[0]
    vt_ones = jnp.concatenate([vt_ref[0], jnp.ones((BF16_SUBLANES, S), BF16)], axis=0)
    lam = (jnp.exp(jnp.sum(lq1_ref[...] * lk1_ref[...], axis=-1, keepdims=True))
           - jnp.exp(jnp.sum(lq2_ref[...] * lk2_ref[...], axis=-1, keepdims=True))
           + LAM_INIT)
    gain = g_ref[...] * (1.0 - LAM_INIT)
    feat = lax.broadcasted_iota(jnp.int32, (2 * HEAD_DIM, rows), 0)
    s_scr = (s0_scr, s1_scr)

    def cols(j):
        return pl.ds(j * rows, rows)

    def scores(j, m):
        qt = qt_ref[0, :, cols(j)]
        in_map = (feat >= m * HEAD_DIM) & (feat < (m + 1) * HEAD_DIM)
        qm = jnp.where(in_map, qt, jnp.zeros_like(qt))
        s = jnp.dot(k, qm, preferred_element_type=F32)
        s_scr[m][...] = s
        return jnp.max(s, axis=0, keepdims=True)

    def numerators(m, mx):
        return jnp.exp2(s_scr[m][...] - mx).astype(BF16)

    def weighted_values(p):
        o = jnp.dot(vt_ones, p, preferred_element_type=F32)
        return o[:2 * HEAD_DIM] / o[2 * HEAD_DIM:2 * HEAD_DIM + 1]

    def finish(j, o0, o1):
        o = o0 - lam * o1
        o = o * lax.rsqrt(jnp.mean(o * o, axis=0, keepdims=True) + SUBLN_EPS) * gain
        od_ref[0, cols(j), :] = o.T.astype(od_ref.dtype)

    qrows = DIL_Q_ROWS
    head0 = lax.broadcasted_iota(jnp.int32, (1, 1, LANES), 2) < HEAD_DIM

    def dilated_scores(j, pi):
        (window, d), bias_ref = DIL_PATTERNS[pi], (b0_ref, b1_ref, b2_ref)[pi]
        radius, L, wk, blocks = _dil_geometry(window, d, S)
        per = d * blocks // nblocks
        qsl, ksl = [], []
        for i in range(per):
            t = j * per + i
            r, q0 = t // blocks, (t % blocks) * qrows
            k0 = min(max(q0 - radius, 0), L - wk)
            qsl.append(pl.ds(q0 * d + r, qrows, stride=d))
            ksl.append(pl.ds(k0 * d + r, wk, stride=d))
        qt = jnp.stack([dq_ref[0, sl, :].astype(BF16) for sl in qsl])
        kt = jnp.stack([dk_ref[0, sl, :].astype(BF16) for sl in ksl])
        vt = jnp.stack([dv_ref[0, sl, :].astype(BF16) for sl in ksl])
        bias = bias_ref[pl.ds(j * per, per)]

        def per_head(x):
            zero = jnp.zeros_like(x)
            return jnp.concatenate([jnp.where(head0, x, zero), jnp.where(head0, zero, x)], axis=1)

        s = jnp.einsum("tqc,tkc->tqk", qt, per_head(kt), preferred_element_type=F32)
        ps, mxs = [], []
        for h in range(2):
            sh = s[:, :, h * wk:(h + 1) * wk] + bias
            mx = jnp.max(sh, axis=-1, keepdims=True)
            ps.append(jnp.exp2(sh - mx).astype(BF16))
            mxs.append(mx)
        mx = jnp.where(head0, mxs[0], mxs[1])
        first_copy = lax.broadcasted_iota(jnp.int32, (2 * wk, LANES), 0) < wk
        first_head = lax.broadcasted_iota(jnp.int32, (2 * wk, LANES), 1) < HEAD_DIM
        ones2 = jnp.where(first_copy == first_head, 1.0, 0.0).astype(BF16)
        v2 = jnp.concatenate([per_head(vt), jnp.broadcast_to(ones2, (per, 2 * wk, LANES))],
                             axis=-1)
        return pi, qsl, v2, jnp.concatenate(ps, axis=-1), mx

    def dilated_values(ctx):
        pi, qsl, v2, p, mx = ctx
        nz = jnp.einsum("tqk,tkc->tqc", p, v2, preferred_element_type=F32)
        for i, dst in enumerate(qsl):
            n_scr[pi, dst, :] = nz[i, :, :LANES]
            m_scr[pi, dst, :] = mx[i]
            z_scr[pi, dst, :] = nz[i, :, LANES:]

    groups = iter([(j, pi) for j in range(nblocks) for pi in range(len(DIL_PATTERNS))])

    def hosted(long_matmul):
        ctx = dilated_scores(*next(groups))
        out = long_matmul()
        dilated_values(ctx)
        return out

    mx0 = hosted(lambda: scores(0, 0))
    for j in range(nblocks):
        p0 = numerators(0, mx0)
        mx1 = hosted(lambda: scores(j, 1))
        o0 = weighted_values(p0)
        p1 = numerators(1, mx1)
        if j + 1 < nblocks:
            mx0 = hosted(lambda: scores(j + 1, 0))
        o1 = hosted(lambda: weighted_values(p1))
        finish(j, o0, o1)

    chunk = 256

    def combine(c, carry):
        sl = pl.ds(pl.multiple_of(c * chunk, chunk), chunk)
        ms = [m_scr[pi, sl, :] for pi in range(len(DIL_PATTERNS))]
        top = functools.reduce(jnp.maximum, ms)
        num = jnp.zeros((chunk, LANES), F32)
        den = jnp.zeros((chunk, LANES), F32)
        for pi in range(len(DIL_PATTERNS)):
            a = jnp.exp2(ms[pi] - top)
            num = num + a * n_scr[pi, sl, :]
            den = den + a * z_scr[pi, sl, :]
        og_ref[0, sl, :] = (num / den).astype(og_ref.dtype)
        return carry

    lax.fori_loop(0, S // chunk, combine, 0)


def _attention(qt, kd, vt, dil_qkv, lq1, lk1, lq2, lk2, g_subln):
    B, S, _ = kd.shape
    nblk = D_DIL // LANES
    assert nblk == N_HEADS_DIFF
    small = lambda n: pl.BlockSpec((1, n), lambda b, h: (0, 0))
    biases = _dil_bias(S)
    const = lambda a: pl.BlockSpec(a.shape, lambda b, h: (0, 0, 0), pipeline_mode=pl.Buffered(1))
    s_scr = pltpu.VMEM((S, DIFF_Q_ROWS), F32)
    stat_scr = pltpu.VMEM((len(DIL_PATTERNS), S, LANES), F32)
    return pl.pallas_call(
        _attn_kernel,
        grid=(B, N_HEADS_DIFF),
        in_specs=[
            pl.BlockSpec((1, LANES, S), lambda b, h: (b, h, 0)),
            pl.BlockSpec((1, S, LANES), lambda b, h: (b, 0, h)),
            pl.BlockSpec((1, LANES, S), lambda b, h: (b, h, 0)),
            small(HEAD_DIM), small(HEAD_DIM), small(HEAD_DIM), small(HEAD_DIM),
            pl.BlockSpec((2 * HEAD_DIM, 1), lambda b, h: (0, 0)),
            pl.BlockSpec((1, S, LANES), lambda b, h: (b, 0, h)),
            pl.BlockSpec((1, S, LANES), lambda b, h: (b, 0, nblk + h)),
            pl.BlockSpec((1, S, LANES), lambda b, h: (b, 0, 2 * nblk + h)),
        ] + [const(a) for a in biases],
        out_specs=[
            pl.BlockSpec((1, S, LANES), lambda b, h: (b, 0, h)),
            pl.BlockSpec((1, S, LANES), lambda b, h: (b, 0, h)),
        ],
        out_shape=[
            jax.ShapeDtypeStruct((B, S, D_DIFF), BF16),
            jax.ShapeDtypeStruct((B, S, D_DIL), BF16),
        ],
        scratch_shapes=[s_scr, s_scr, stat_scr, stat_scr, stat_scr],
        compiler_params=pltpu.CompilerParams(
            dimension_semantics=("parallel", "parallel"),
            vmem_limit_bytes=VMEM_LIMIT_BYTES),
        name="attention",
    )(qt, kd, vt, lq1, lk1, lq2, lk2, g_subln.reshape(2 * HEAD_DIM, 1),
      dil_qkv, dil_qkv, dil_qkv, *biases)


def _post_kernel(x_ref, od_ref, og_ref, p_ref, wo_ref, gm_ref, wu_ref, wd_ref,
                 gp_ref, wg_ref, wp_ref, gf_ref, o_ref):
    half_rows = x_ref.shape[0] // POST_HALVES
    halves = [pl.ds(i * half_rows, half_rows) for i in range(POST_HALVES)]

    h = []
    for sl in halves:
        mixed = jnp.concatenate([od_ref[sl, :], og_ref[sl, :]], axis=-1)
        h.append(x_ref[sl, :] + jnp.dot(mixed, wo_ref[...], preferred_element_type=F32))

    for i in range(POST_HALVES):
        hn = _rms(h[i], gm_ref[...], NORM_EPS).astype(BF16)
        for c in range(D_FF // FF_CHUNK):
            u = jnp.dot(hn, wu_ref[:, c * FF_CHUNK:(c + 1) * FF_CHUNK], preferred_element_type=F32)
            u = jnp.square(jnp.maximum(u, 0.0)).astype(BF16)
            h[i] = h[i] + jnp.dot(u, wd_ref[c * FF_CHUNK:(c + 1) * FF_CHUNK, :],
                                  preferred_element_type=F32)

    for i, sl in enumerate(halves):
        hn = _rms(h[i], gp_ref[...], NORM_EPS).astype(BF16)
        gate = jax.nn.sigmoid(jnp.dot(hn, wg_ref[...], preferred_element_type=F32))
        emb = jnp.dot(p_ref[sl, :].astype(BF16), wp_ref[...], preferred_element_type=F32)
        o_ref[sl, :] = _rms(h[i] + gate * emb, gf_ref[...], NORM_EPS)


def _post(x2, od2, og2, p2, w_out, g_mlp, w_up, w_down, g_ple, w_gate, w_ple, g_final):
    N, D = x2.shape
    rows = POST_ROWS
    tok = lambda width: pl.BlockSpec((rows, width), lambda i: (i, 0))
    whole = lambda a: pl.BlockSpec(a.shape, lambda i: (0, 0), pipeline_mode=pl.Buffered(1))
    return pl.pallas_call(
        _post_kernel,
        grid=(N // rows,),
        in_specs=[
            tok(D), tok(D_DIFF), tok(D_DIL), tok(D_PLE),
            whole(w_out), whole(g_mlp), whole(w_up), whole(w_down),
            whole(g_ple), whole(w_gate), whole(w_ple), whole(g_final),
        ],
        out_specs=tok(D),
        out_shape=jax.ShapeDtypeStruct((N, D), F32),
        compiler_params=pltpu.CompilerParams(
            dimension_semantics=("parallel",),
            vmem_limit_bytes=VMEM_LIMIT_BYTES),
        name="out_mlp_ple",
    )(x2, od2, og2, p2, w_out, g_mlp, w_up, w_down, g_ple, w_gate, w_ple, g_final)


def _rope_tables(seq):
    inv = ROPE_THETA ** (-jnp.arange(0, HEAD_DIM, 2, dtype=F32) / HEAD_DIM)
    ang = jnp.arange(seq, dtype=F32)[:, None] * inv[None, :]
    cos, sin = jnp.cos(ang), jnp.sin(ang)
    reps = LANES // HEAD_DIM
    cos_t = jnp.tile(jnp.concatenate([cos, cos], axis=-1), (1, reps))
    sin_t = jnp.tile(jnp.concatenate([-sin, sin], axis=-1), (1, reps))
    return cos_t, sin_t, cos.T, sin.T


def kernel(x, p, w_in, w_out, g_mix, lambda_q1, lambda_k1, lambda_q2, lambda_k2, g_subln,
           g_mlp, w_up, w_down, g_ple, w_ple_gate, w_ple_proj, g_final):
    B, S, D = x.shape
    assert x.shape[1:] == (2048, D_MODEL) and w_in.shape == (1, D_MODEL, 2 * D_GROUP)

    qt, kd, vt, dil_qkv = _project(x, g_mix, w_in[0], _rope_tables(S))
    od, og = _attention(qt, kd, vt, dil_qkv, lambda_q1, lambda_k1, lambda_q2, lambda_k2, g_subln)

    out = _post(
        x.reshape(B * S, D), od.reshape(B * S, D_DIFF), og.reshape(B * S, D_DIL),
        p[0].reshape(B * S, D_PLE),
        w_out[0].astype(BF16), g_mlp, w_up[0].astype(BF16), w_down[0].astype(BF16),
        g_ple, w_ple_gate[0].astype(BF16), w_ple_proj[0].astype(BF16), g_final.reshape(1, D))
    return out.reshape(B, S, D)
```

```python
import functools
import math

import jax
import jax.numpy as jnp
import numpy as np
from jax import lax
from jax.experimental import pallas as pl
from jax.experimental.pallas import tpu as pltpu

D_MODEL = 1024
HEAD_DIM = 64
D_DIFF = 512
D_DIL = 512
D_GROUP = 3 * D_DIFF
N_HEADS_DIFF = 4
DIL_PATTERNS = ((128, 1), (512, 4), (2048, 16))
D_FF = 4 * D_MODEL
D_PLE = 256
ROPE_THETA = 10000.0
NORM_EPS = 1e-6
SUBLN_EPS = 1e-5
NEG_INF = -1e30
LAM_INIT = 0.8 - 0.6 * math.exp(-0.3 * 0)
LOG2_E = math.log2(math.e)

LANES = 128
BF16_SUBLANES = 16
VMEM_LIMIT_BYTES = 56 * 1024 * 1024

PROJ_ROWS = 1024
PROJ_HALVES = 2
DIFF_Q_ROWS = 512
DIL_Q_ROWS = 128
POST_ROWS = 512
POST_HALVES = 2
FF_CHUNK = 1024

BF16 = jnp.bfloat16
F32 = jnp.float32
NT_DIMS = (((1,), (1,)), ((), ()))


def _rms(x, g, eps):
    return x * lax.rsqrt(jnp.mean(x * x, axis=-1, keepdims=True) + eps) * g


def _proj_kernel(n_cast, x_ref, g_ref, wqt_ref, wvt_ref, w_ref, cos_ref, sin_ref, cost_ref, sint_ref,
                 *rest):
    cast_in, (qt_ref, kd_ref, vt_ref, dil_ref), cast_out = rest[:n_cast], rest[n_cast:n_cast + 4], rest[n_cast + 4:]
    for src_ref, dst_ref in zip(cast_in, cast_out):
        dst_ref[...] = src_ref[...].astype(dst_ref.dtype)

    scale = HEAD_DIM ** -0.5 * LOG2_E
    half = HEAD_DIM // 2
    sub_rows = x_ref.shape[1] // PROJ_HALVES
    lane = lax.broadcasted_iota(jnp.int32, (sub_rows, LANES), 1)
    first_half = (lane % HEAD_DIM) < half
    slabs = ((kd_ref, 0, True, False), (dil_ref, 0, True, True),
             (dil_ref, D_DIL, True, False), (dil_ref, 2 * D_DIL, False, False))

    for i in range(PROJ_HALVES):
        sl = pl.ds(i * sub_rows, sub_rows)
        hn = _rms(x_ref[0, sl, :], g_ref[...], NORM_EPS).astype(BF16)
        cos = cos_ref[sl, :]
        sin = sin_ref[sl, :]

        def rope(t, cos=cos, sin=sin):
            swapped = jnp.where(first_half,
                                pltpu.roll(t, LANES - half, 1),
                                pltpu.roll(t, half, 1))
            return t * cos + swapped * sin

        qt = lax.dot_general(wqt_ref[...], hn, NT_DIMS, preferred_element_type=F32)
        cos_t = cost_ref[:, sl] * scale
        sin_t = sint_ref[:, sl] * scale
        for hb in range(D_DIFF // HEAD_DIM):
            t1 = qt[hb * HEAD_DIM:hb * HEAD_DIM + half]
            t2 = qt[hb * HEAD_DIM + half:(hb + 1) * HEAD_DIM]
            qt_ref[0, hb * HEAD_DIM:hb * HEAD_DIM + half, sl] = (t1 * cos_t - t2 * sin_t).astype(BF16)
            qt_ref[0, hb * HEAD_DIM + half:(hb + 1) * HEAD_DIM, sl] = (t1 * sin_t + t2 * cos_t).astype(BF16)

        vt_ref[0, :, sl] = lax.dot_general(wvt_ref[...], hn, NT_DIMS,
                                           preferred_element_type=F32).astype(BF16)

        for n, (out_ref, o0, use_rope, use_scale) in enumerate(slabs):
            acc = jnp.dot(hn, w_ref[:, n * D_DIFF:(n + 1) * D_DIFF], preferred_element_type=F32)
            for c in range(D_DIFF // LANES):
                t = acc[:, c * LANES:(c + 1) * LANES]
                if use_rope:
                    t = rope(t)
                if use_scale:
                    t = t * scale
                out_ref[0, sl, o0 + c * LANES:o0 + (c + 1) * LANES] = t.astype(out_ref.dtype)


def _project(x, g_mix, w_in, tables, later_weights):
    B, S, D = x.shape
    rows = PROJ_ROWS
    steps = B * (S // rows)
    slab = lambda w: pl.BlockSpec((w.shape[0] // steps, w.shape[1]), lambda b, i: (b * (S // rows) + i, 0))
    cos_t, sin_t, cos_tt, sin_tt = tables
    wqt = w_in[:, :D_DIFF].T.astype(BF16)
    wvt = w_in[:, 2 * D_DIFF:3 * D_DIFF].T.astype(BF16)
    w_rest = jnp.concatenate([w_in[:, D_DIFF:2 * D_DIFF], w_in[:, D_GROUP:]], axis=1).astype(BF16)
    const = lambda a: pl.BlockSpec(a.shape, lambda b, i: (0, 0), pipeline_mode=pl.Buffered(1))
    outs = pl.pallas_call(
        functools.partial(_proj_kernel, len(later_weights)),
        grid=(B, S // rows),
        in_specs=[
            pl.BlockSpec((1, rows, D), lambda b, i: (b, i, 0)),
            pl.BlockSpec((1, D), lambda b, i: (0, 0)),
            const(wqt), const(wvt), const(w_rest),
            pl.BlockSpec((rows, LANES), lambda b, i: (i, 0)),
            pl.BlockSpec((rows, LANES), lambda b, i: (i, 0)),
            pl.BlockSpec((HEAD_DIM // 2, rows), lambda b, i: (0, i)),
            pl.BlockSpec((HEAD_DIM // 2, rows), lambda b, i: (0, i)),
        ] + [slab(w) for w in later_weights],
        out_specs=[
            pl.BlockSpec((1, D_DIFF, rows), lambda b, i: (b, 0, i)),
            pl.BlockSpec((1, rows, D_DIFF), lambda b, i: (b, i, 0)),
            pl.BlockSpec((1, D_DIFF, rows), lambda b, i: (b, 0, i)),
            pl.BlockSpec((1, rows, D_GROUP), lambda b, i: (b, i, 0)),
        ] + [slab(w) for w in later_weights],
        out_shape=[
            jax.ShapeDtypeStruct((B, D_DIFF, S), BF16),
            jax.ShapeDtypeStruct((B, S, D_DIFF), BF16),
            jax.ShapeDtypeStruct((B, D_DIFF, S), BF16),
            jax.ShapeDtypeStruct((B, S, D_GROUP), F32),
        ] + [jax.ShapeDtypeStruct(w.shape, BF16) for w in later_weights],
        compiler_params=pltpu.CompilerParams(
            dimension_semantics=("parallel", "parallel"),
            vmem_limit_bytes=VMEM_LIMIT_BYTES),
        name="proj_rope",
    )(x, g_mix, wqt, wvt, w_rest, cos_t, sin_t, cos_tt, sin_tt, *later_weights)
    return outs[:4], outs[4:]


def _dil_geometry(window, d, seq):
    radius, L = window // (2 * d), seq // d
    wk = min(DIL_Q_ROWS + 2 * radius, L)
    return radius, L, wk, L // DIL_Q_ROWS


def _dil_bias(seq):
    out = []
    for window, d in DIL_PATTERNS:
        radius, L, wk, blocks = _dil_geometry(window, d, seq)
        row = np.arange(DIL_Q_ROWS)[:, None]
        col = np.arange(wk)[None, :]
        bias = []
        for t in range(d * blocks):
            q0 = (t % blocks) * DIL_Q_ROWS
            k0 = min(max(q0 - radius, 0), L - wk)
            bias.append(np.where(np.abs((k0 + col) - (q0 + row)) <= radius, 0.0, NEG_INF))
        out.append(jnp.asarray(np.stack(bias), dtype=F32))
    return out


def _attn_kernel(qt_ref, k_ref, vt_ref, lq1_ref, lk1_ref, lq2_ref, lk2_ref, g_ref,
                 dq_ref, dk_ref, dv_ref, b0_ref, b1_ref, b2_ref,
                 od_ref, og_ref, s0_scr, s1_scr, n_scr, m_scr, z_scr):
    S = k_ref.shape[1]
    rows = DIFF_Q_ROWS
    nblocks = S // rows
    k = k_ref[0]
    vt_ones = jnp.concatenate([vt_ref[0], jnp.ones((BF16_SUBLANES, S), BF16)], axis=0)
    lam = (jnp.exp(jnp.sum(lq1_ref[...] * lk1_ref[...], axis=-1, keepdims=True))
           - jnp.exp(jnp.sum(lq2_ref[...] * lk2_ref[...], axis=-1, keepdims=True))
           + LAM_INIT)
    gain = g_ref[...] * (1.0 - LAM_INIT)
    feat = lax.broadcasted_iota(jnp.int32, (2 * HEAD_DIM, rows), 0)
    s_scr = (s0_scr, s1_scr)

    def cols(j):
        return pl.ds(j * rows, rows)

    def scores(j, m):
        qt = qt_ref[0, :, cols(j)]
        in_map = (feat >= m * HEAD_DIM) & (feat < (m + 1) * HEAD_DIM)
        qm = jnp.where(in_map, qt, jnp.zeros_like(qt))
        s = jnp.dot(k, qm, preferred_element_type=F32)
        s_scr[m][...] = s
        return jnp.max(s, axis=0, keepdims=True)

    def numerators(m, mx):
        return jnp.exp2(s_scr[m][...] - mx).astype(BF16)

    def weighted_values(p):
        o = jnp.dot(vt_ones, p, preferred_element_type=F32)
        return o[:2 * HEAD_DIM] / o[2 * HEAD_DIM:2 * HEAD_DIM + 1]

    def finish(j, o0, o1):
        o = o0 - lam * o1
        o = o * lax.rsqrt(jnp.mean(o * o, axis=0, keepdims=True) + SUBLN_EPS) * gain
        od_ref[0, cols(j), :] = o.T.astype(od_ref.dtype)

    qrows = DIL_Q_ROWS
    head0 = lax.broadcasted_iota(jnp.int32, (1, 1, LANES), 2) < HEAD_DIM

    def dilated_scores(j, pi):
        (window, d), bias_ref = DIL_PATTERNS[pi], (b0_ref, b1_ref, b2_ref)[pi]
        radius, L, wk, blocks = _dil_geometry(window, d, S)
        per = d * blocks // nblocks
        qsl, ksl = [], []
        for i in range(per):
            t = j * per + i
            r, q0 = t // blocks, (t % blocks) * qrows
            k0 = min(max(q0 - radius, 0), L - wk)
            qsl.append(pl.ds(q0 * d + r, qrows, stride=d))
            ksl.append(pl.ds(k0 * d + r, wk, stride=d))
        qt = jnp.stack([dq_ref[0, sl, :].astype(BF16) for sl in qsl])
        kt = jnp.stack([dk_ref[0, sl, :].astype(BF16) for sl in ksl])
        vt = jnp.stack([dv_ref[0, sl, :].astype(BF16) for sl in ksl])
        bias = bias_ref[pl.ds(j * per, per)]

        def per_head(x):
            zero = jnp.zeros_like(x)
            return jnp.concatenate([jnp.where(head0, x, zero), jnp.where(head0, zero, x)], axis=1)

        s = jnp.einsum("tqc,tkc->tqk", qt, per_head(kt), preferred_element_type=F32)
        ps, mxs, zs = [], [], []
        for h in range(2):
            sh = s[:, :, h * wk:(h + 1) * wk] + bias
            mx = jnp.max(sh, axis=-1, keepdims=True)
            p = jnp.exp2(sh - mx)
            ps.append(p.astype(BF16))
            mxs.append(mx)
            zs.append(jnp.sum(p, axis=-1, keepdims=True))
        mx = jnp.where(head0, mxs[0], mxs[1])
        z = jnp.where(head0, zs[0], zs[1])
        return pi, qsl, per_head(vt), jnp.concatenate(ps, axis=-1), mx, z

    def dilated_values(ctx):
        pi, qsl, v2, p, mx, z = ctx
        n = jnp.einsum("tqk,tkc->tqc", p, v2, preferred_element_type=F32)
        for i, dst in enumerate(qsl):
            n_scr[pi, dst, :] = n[i]
            m_scr[pi, dst, :] = mx[i]
            z_scr[pi, dst, :] = z[i]

    groups = iter([(j, pi) for j in range(nblocks) for pi in range(len(DIL_PATTERNS))])

    def hosted(long_matmul):
        ctx = dilated_scores(*next(groups))
        out = long_matmul()
        dilated_values(ctx)
        return out

    mx0 = hosted(lambda: scores(0, 0))
    for j in range(nblocks):
        p0 = numerators(0, mx0)
        mx1 = hosted(lambda: scores(j, 1))
        o0 = weighted_values(p0)
        p1 = numerators(1, mx1)
        if j + 1 < nblocks:
            mx0 = hosted(lambda: scores(j + 1, 0))
        o1 = hosted(lambda: weighted_values(p1))
        finish(j, o0, o1)

    chunk = 256

    def combine(c, carry):
        sl = pl.ds(pl.multiple_of(c * chunk, chunk), chunk)
        ms = [m_scr[pi, sl, :] for pi in range(len(DIL_PATTERNS))]
        top = functools.reduce(jnp.maximum, ms)
        num = jnp.zeros((chunk, LANES), F32)
        den = jnp.zeros((chunk, LANES), F32)
        for pi in range(len(DIL_PATTERNS)):
            a = jnp.exp2(ms[pi] - top)
            num = num + a * n_scr[pi, sl, :]
            den = den + a * z_scr[pi, sl, :]
        og_ref[0, sl, :] = (num / den).astype(og_ref.dtype)
        return carry

    lax.fori_loop(0, S // chunk, combine, 0)


def _attention(qt, kd, vt, dil_qkv, lq1, lk1, lq2, lk2, g_subln):
    B, S, _ = kd.shape
    nblk = D_DIL // LANES
    assert nblk == N_HEADS_DIFF
    small = lambda n: pl.BlockSpec((1, n), lambda b, h: (0, 0))
    biases = _dil_bias(S)
    const = lambda a: pl.BlockSpec(a.shape, lambda b, h: (0, 0, 0), pipeline_mode=pl.Buffered(1))
    s_scr = pltpu.VMEM((S, DIFF_Q_ROWS), F32)
    stat_scr = pltpu.VMEM((len(DIL_PATTERNS), S, LANES), F32)
    return pl.pallas_call(
        _attn_kernel,
        grid=(B, N_HEADS_DIFF),
        in_specs=[
            pl.BlockSpec((1, LANES, S), lambda b, h: (b, h, 0)),
            pl.BlockSpec((1, S, LANES), lambda b, h: (b, 0, h)),
            pl.BlockSpec((1, LANES, S), lambda b, h: (b, h, 0)),
            small(HEAD_DIM), small(HEAD_DIM), small(HEAD_DIM), small(HEAD_DIM),
            pl.BlockSpec((2 * HEAD_DIM, 1), lambda b, h: (0, 0)),
            pl.BlockSpec((1, S, LANES), lambda b, h: (b, 0, h)),
            pl.BlockSpec((1, S, LANES), lambda b, h: (b, 0, nblk + h)),
            pl.BlockSpec((1, S, LANES), lambda b, h: (b, 0, 2 * nblk + h)),
        ] + [const(a) for a in biases],
        out_specs=[
            pl.BlockSpec((1, S, LANES), lambda b, h: (b, 0, h)),
            pl.BlockSpec((1, S, LANES), lambda b, h: (b, 0, h)),
        ],
        out_shape=[
            jax.ShapeDtypeStruct((B, S, D_DIFF), BF16),
            jax.ShapeDtypeStruct((B, S, D_DIL), BF16),
        ],
        scratch_shapes=[s_scr, s_scr, stat_scr, stat_scr, stat_scr],
        compiler_params=pltpu.CompilerParams(
            dimension_semantics=("parallel", "parallel"),
            vmem_limit_bytes=VMEM_LIMIT_BYTES),
        name="attention",
    )(qt, kd, vt, lq1, lk1, lq2, lk2, g_subln.reshape(2 * HEAD_DIM, 1),
      dil_qkv, dil_qkv, dil_qkv, *biases)


def _post_kernel(x_ref, od_ref, og_ref, p_ref, wo_ref, gm_ref, wu_ref, wd_ref,
                 gp_ref, wg_ref, wp_ref, gf_ref, o_ref):
    half_rows = x_ref.shape[0] // POST_HALVES
    halves = [pl.ds(i * half_rows, half_rows) for i in range(POST_HALVES)]

    h = []
    for sl in halves:
        mixed = jnp.concatenate([od_ref[sl, :], og_ref[sl, :]], axis=-1)
        h.append(x_ref[sl, :] + jnp.dot(mixed, wo_ref[...], preferred_element_type=F32))

    for i in range(POST_HALVES):
        hn = _rms(h[i], gm_ref[...], NORM_EPS).astype(BF16)
        for c in range(D_FF // FF_CHUNK):
            u = jnp.dot(hn, wu_ref[:, c * FF_CHUNK:(c + 1) * FF_CHUNK], preferred_element_type=F32)
            u = jnp.square(jnp.maximum(u, 0.0)).astype(BF16)
            h[i] = h[i] + jnp.dot(u, wd_ref[c * FF_CHUNK:(c + 1) * FF_CHUNK, :],
                                  preferred_element_type=F32)

    for i, sl in enumerate(halves):
        hn = _rms(h[i], gp_ref[...], NORM_EPS).astype(BF16)
        gate = jax.nn.sigmoid(jnp.dot(hn, wg_ref[...], preferred_element_type=F32))
        emb = jnp.dot(p_ref[sl, :].astype(BF16), wp_ref[...], preferred_element_type=F32)
        o_ref[sl, :] = _rms(h[i] + gate * emb, gf_ref[...], NORM_EPS)


def _post(x2, od2, og2, p2, w_out, g_mlp, w_up, w_down, g_ple, w_gate, w_ple, g_final):
    N, D = x2.shape
    rows = POST_ROWS
    tok = lambda width: pl.BlockSpec((rows, width), lambda i: (i, 0))
    whole = lambda a: pl.BlockSpec(a.shape, lambda i: (0, 0), pipeline_mode=pl.Buffered(1))
    return pl.pallas_call(
        _post_kernel,
        grid=(N // rows,),
        in_specs=[
            tok(D), tok(D_DIFF), tok(D_DIL), tok(D_PLE),
            whole(w_out), whole(g_mlp), whole(w_up), whole(w_down),
            whole(g_ple), whole(w_gate), whole(w_ple), whole(g_final),
        ],
        out_specs=tok(D),
        out_shape=jax.ShapeDtypeStruct((N, D), F32),
        compiler_params=pltpu.CompilerParams(
            dimension_semantics=("parallel",),
            vmem_limit_bytes=VMEM_LIMIT_BYTES),
        name="out_mlp_ple",
    )(x2, od2, og2, p2, w_out, g_mlp, w_up, w_down, g_ple, w_gate, w_ple, g_final)


def _rope_tables(seq):
    inv = ROPE_THETA ** (-jnp.arange(0, HEAD_DIM, 2, dtype=F32) / HEAD_DIM)
    ang = jnp.arange(seq, dtype=F32)[:, None] * inv[None, :]
    cos, sin = jnp.cos(ang), jnp.sin(ang)
    reps = LANES // HEAD_DIM
    cos_t = jnp.tile(jnp.concatenate([cos, cos], axis=-1), (1, reps))
    sin_t = jnp.tile(jnp.concatenate([-sin, sin], axis=-1), (1, reps))
    return cos_t, sin_t, cos.T, sin.T


def kernel(x, p, w_in, w_out, g_mix, lambda_q1, lambda_k1, lambda_q2, lambda_k2, g_subln,
           g_mlp, w_up, w_down, g_ple, w_ple_gate, w_ple_proj, g_final):
    B, S, D = x.shape
    assert x.shape[1:] == (2048, D_MODEL) and w_in.shape == (1, D_MODEL, 2 * D_GROUP)

    later = (w_out[0], w_up[0], w_down[0], w_ple_gate[0], w_ple_proj[0])
    (qt, kd, vt, dil_qkv), (wo, wu, wd, wg, wp) = _project(x, g_mix, w_in[0], _rope_tables(S), later)
    od, og = _attention(qt, kd, vt, dil_qkv, lambda_q1, lambda_k1, lambda_q2, lambda_k2, g_subln)

    out = _post(
        x.reshape(B * S, D), od.reshape(B * S, D_DIFF), og.reshape(B * S, D_DIL),
        p[0].reshape(B * S, D_PLE),
        wo, g_mlp, wu, wd, g_ple, wg, wp, g_final.reshape(1, D))
    return out.reshape(B, S, D)
```

```python
import functools
import math

import jax
import jax.numpy as jnp
import numpy as np
from jax import lax
from jax.experimental import pallas as pl
from jax.experimental.pallas import tpu as pltpu

D_MODEL = 1024
HEAD_DIM = 64
D_DIFF = 512
D_DIL = 512
D_GROUP = 3 * D_DIFF
N_HEADS_DIFF = 4
DIL_PATTERNS = ((128, 1), (512, 4), (2048, 16))
D_FF = 4 * D_MODEL
D_PLE = 256
ROPE_THETA = 10000.0
NORM_EPS = 1e-6
SUBLN_EPS = 1e-5
NEG_INF = -1e30
LAM_INIT = 0.8 - 0.6 * math.exp(-0.3 * 0)
LOG2_E = math.log2(math.e)

LANES = 128
BF16_SUBLANES = 16
VMEM_LIMIT_BYTES = 56 * 1024 * 1024

PROJ_ROWS = 1024
PROJ_HALVES = 2
DIFF_Q_ROWS = 512
DIL_Q_ROWS = 128
POST_ROWS = 512
POST_HALVES = 2
FF_CHUNK = 1024

BF16 = jnp.bfloat16
F32 = jnp.float32
NT_DIMS = (((1,), (1,)), ((), ()))


def _rms(x, g, eps):
    return x * lax.rsqrt(jnp.mean(x * x, axis=-1, keepdims=True) + eps) * g


def _proj_kernel(n_cast, x_ref, g_ref, wqt_ref, wvt_ref, w_ref, cos_ref, sin_ref, cost_ref, sint_ref,
                 *rest):
    cast_in, (qt_ref, kd_ref, vt_ref, dil_ref), cast_out = rest[:n_cast], rest[n_cast:n_cast + 4], rest[n_cast + 4:]
    for src_ref, dst_ref in zip(cast_in, cast_out):
        dst_ref[...] = src_ref[...].astype(dst_ref.dtype)

    scale = HEAD_DIM ** -0.5 * LOG2_E
    half = HEAD_DIM // 2
    sub_rows = x_ref.shape[1] // PROJ_HALVES
    lane = lax.broadcasted_iota(jnp.int32, (sub_rows, LANES), 1)
    first_half = (lane % HEAD_DIM) < half
    slabs = ((kd_ref, 0, True, False), (dil_ref, 0, True, True),
             (dil_ref, D_DIL, True, False), (dil_ref, 2 * D_DIL, False, False))

    for i in range(PROJ_HALVES):
        sl = pl.ds(i * sub_rows, sub_rows)
        hn = _rms(x_ref[0, sl, :], g_ref[...], NORM_EPS).astype(BF16)
        cos = cos_ref[sl, :]
        sin = sin_ref[sl, :]

        def rope(t, cos=cos, sin=sin):
            swapped = jnp.where(first_half,
                                pltpu.roll(t, LANES - half, 1),
                                pltpu.roll(t, half, 1))
            return t * cos + swapped * sin

        qt = lax.dot_general(wqt_ref[...], hn, NT_DIMS, preferred_element_type=F32)
        cos_t = cost_ref[:, sl] * scale
        sin_t = sint_ref[:, sl] * scale
        for hb in range(D_DIFF // HEAD_DIM):
            t1 = qt[hb * HEAD_DIM:hb * HEAD_DIM + half]
            t2 = qt[hb * HEAD_DIM + half:(hb + 1) * HEAD_DIM]
            qt_ref[0, hb * HEAD_DIM:hb * HEAD_DIM + half, sl] = (t1 * cos_t - t2 * sin_t).astype(BF16)
            qt_ref[0, hb * HEAD_DIM + half:(hb + 1) * HEAD_DIM, sl] = (t1 * sin_t + t2 * cos_t).astype(BF16)

        vt_ref[0, :, sl] = lax.dot_general(wvt_ref[...], hn, NT_DIMS,
                                           preferred_element_type=F32).astype(BF16)

        for n, (out_ref, o0, use_rope, use_scale) in enumerate(slabs):
            acc = jnp.dot(hn, w_ref[:, n * D_DIFF:(n + 1) * D_DIFF], preferred_element_type=F32)
            for c in range(D_DIFF // LANES):
                t = acc[:, c * LANES:(c + 1) * LANES]
                if use_rope:
                    t = rope(t)
                if use_scale:
                    t = t * scale
                out_ref[0, sl, o0 + c * LANES:o0 + (c + 1) * LANES] = t.astype(out_ref.dtype)


def _project(x, g_mix, w_in, tables, later_weights):
    B, S, D = x.shape
    rows = PROJ_ROWS
    steps = B * (S // rows)
    slab = lambda w: pl.BlockSpec((w.shape[0] // steps, w.shape[1]), lambda b, i: (b * (S // rows) + i, 0))
    cos_t, sin_t, cos_tt, sin_tt = tables
    wqt = w_in[:, :D_DIFF].T.astype(BF16)
    wvt = w_in[:, 2 * D_DIFF:3 * D_DIFF].T.astype(BF16)
    w_rest = jnp.concatenate([w_in[:, D_DIFF:2 * D_DIFF], w_in[:, D_GROUP:]], axis=1).astype(BF16)
    const = lambda a: pl.BlockSpec(a.shape, lambda b, i: (0, 0), pipeline_mode=pl.Buffered(1))
    outs = pl.pallas_call(
        functools.partial(_proj_kernel, len(later_weights)),
        grid=(B, S // rows),
        in_specs=[
            pl.BlockSpec((1, rows, D), lambda b, i: (b, i, 0)),
            pl.BlockSpec((1, D), lambda b, i: (0, 0)),
            const(wqt), const(wvt), const(w_rest),
            pl.BlockSpec((rows, LANES), lambda b, i: (i, 0)),
            pl.BlockSpec((rows, LANES), lambda b, i: (i, 0)),
            pl.BlockSpec((HEAD_DIM // 2, rows), lambda b, i: (0, i)),
            pl.BlockSpec((HEAD_DIM // 2, rows), lambda b, i: (0, i)),
        ] + [slab(w) for w in later_weights],
        out_specs=[
            pl.BlockSpec((1, D_DIFF, rows), lambda b, i: (b, 0, i)),
            pl.BlockSpec((1, rows, D_DIFF), lambda b, i: (b, i, 0)),
            pl.BlockSpec((1, D_DIFF, rows), lambda b, i: (b, 0, i)),
            pl.BlockSpec((1, rows, D_GROUP), lambda b, i: (b, i, 0)),
        ] + [slab(w) for w in later_weights],
        out_shape=[
            jax.ShapeDtypeStruct((B, D_DIFF, S), BF16),
            jax.ShapeDtypeStruct((B, S, D_DIFF), BF16),
            jax.ShapeDtypeStruct((B, D_DIFF, S), BF16),
            jax.ShapeDtypeStruct((B, S, D_GROUP), F32),
        ] + [jax.ShapeDtypeStruct(w.shape, BF16) for w in later_weights],
        compiler_params=pltpu.CompilerParams(
            dimension_semantics=("parallel", "parallel"),
            vmem_limit_bytes=VMEM_LIMIT_BYTES),
        name="proj_rope",
    )(x, g_mix, wqt, wvt, w_rest, cos_t, sin_t, cos_tt, sin_tt, *later_weights)
    return outs[:4], outs[4:]


def _dil_geometry(window, d, seq):
    radius, L = window // (2 * d), seq // d
    wk = min(DIL_Q_ROWS + 2 * radius, L)
    return radius, L, wk, L // DIL_Q_ROWS


def _dil_bias(seq):
    out = []
    for window, d in DIL_PATTERNS:
        radius, L, wk, blocks = _dil_geometry(window, d, seq)
        row = np.arange(DIL_Q_ROWS)[:, None]
        col = np.arange(wk)[None, :]
        bias = []
        for t in range(d * blocks):
            q0 = (t % blocks) * DIL_Q_ROWS
            k0 = min(max(q0 - radius, 0), L - wk)
            bias.append(np.where(np.abs((k0 + col) - (q0 + row)) <= radius, 0.0, NEG_INF))
        out.append(jnp.asarray(np.stack(bias), dtype=F32))
    return out


def _attn_kernel(qt_ref, k_ref, vt_ref, lq1_ref, lk1_ref, lq2_ref, lk2_ref, g_ref,
                 dq_ref, dk_ref, dv_ref, b0_ref, b1_ref, b2_ref,
                 od_ref, og_ref, s0_scr, s1_scr, n_scr, m_scr, z_scr):
    S = k_ref.shape[1]
    rows = DIFF_Q_ROWS
    nblocks = S // rows
    k = k_ref[0]
    vt_ones = jnp.concatenate([vt_ref[0], jnp.ones((BF16_SUBLANES, S), BF16)], axis=0)
    lam = (jnp.exp(jnp.sum(lq1_ref[...] * lk1_ref[...], axis=-1, keepdims=True))
           - jnp.exp(jnp.sum(lq2_ref[...] * lk2_ref[...], axis=-1, keepdims=True))
           + LAM_INIT)
    gain = g_ref[...] * (1.0 - LAM_INIT)
    feat = lax.broadcasted_iota(jnp.int32, (2 * HEAD_DIM, rows), 0)
    s_scr = (s0_scr, s1_scr)

    def cols(j):
        return pl.ds(j * rows, rows)

    def scores(j, m):
        qt = qt_ref[0, :, cols(j)]
        in_map = (feat >= m * HEAD_DIM) & (feat < (m + 1) * HEAD_DIM)
        qm = jnp.where(in_map, qt, jnp.zeros_like(qt))
        s = jnp.dot(k, qm, preferred_element_type=F32)
        s_scr[m][...] = s
        return jnp.max(s, axis=0, keepdims=True)

    def numerators(m, mx):
        return jnp.exp2(s_scr[m][...] - mx).astype(BF16)

    def weighted_values(p):
        o = jnp.dot(vt_ones, p, preferred_element_type=F32)
        return o[:2 * HEAD_DIM] / o[2 * HEAD_DIM:2 * HEAD_DIM + 1]

    def finish(j, o0, o1):
        o = o0 - lam * o1
        o = o * lax.rsqrt(jnp.mean(o * o, axis=0, keepdims=True) + SUBLN_EPS) * gain
        od_ref[0, cols(j), :] = o.T.astype(od_ref.dtype)

    qrows = DIL_Q_ROWS
    head0 = lax.broadcasted_iota(jnp.int32, (1, 1, LANES), 2) < HEAD_DIM

    def dilated_scores(j, pi):
        (window, d), bias_ref = DIL_PATTERNS[pi], (b0_ref, b1_ref, b2_ref)[pi]
        radius, L, wk, blocks = _dil_geometry(window, d, S)
        per = d * blocks // nblocks
        qsl, ksl = [], []
        for i in range(per):
            t = j * per + i
            r, q0 = t // blocks, (t % blocks) * qrows
            k0 = min(max(q0 - radius, 0), L - wk)
            qsl.append(pl.ds(q0 * d + r, qrows, stride=d))
            ksl.append(pl.ds(k0 * d + r, wk, stride=d))
        qt = jnp.stack([dq_ref[0, sl, :].astype(BF16) for sl in qsl])
        kt = jnp.stack([dk_ref[0, sl, :].astype(BF16) for sl in ksl])
        vt = jnp.stack([dv_ref[0, sl, :].astype(BF16) for sl in ksl])
        bias = bias_ref[pl.ds(j * per, per)]

        def per_head(x):
            zero = jnp.zeros_like(x)
            return jnp.concatenate([jnp.where(head0, x, zero), jnp.where(head0, zero, x)], axis=1)

        s = jnp.einsum("tqc,tkc->tqk", qt, per_head(kt), preferred_element_type=F32)
        ps, mxs, zs = [], [], []
        for h in range(2):
            sh = s[:, :, h * wk:(h + 1) * wk] + bias
            mx = jnp.max(sh, axis=-1, keepdims=True)
            p = jnp.exp2(sh - mx)
            ps.append(p.astype(BF16))
            mxs.append(mx)
            zs.append(jnp.sum(p, axis=-1, keepdims=True))
        mx = jnp.where(head0, mxs[0], mxs[1])
        z = jnp.where(head0, zs[0], zs[1])
        return pi, qsl, per_head(vt), jnp.concatenate(ps, axis=-1), mx, z

    def dilated_values(ctx):
        pi, qsl, v2, p, mx, z = ctx
        n = jnp.einsum("tqk,tkc->tqc", p, v2, preferred_element_type=F32)
        for i, dst in enumerate(qsl):
            n_scr[pi, dst, :] = n[i]
            m_scr[pi, dst, :] = mx[i]
            z_scr[pi, dst, :] = z[i]

    groups = iter([(j, pi) for j in range(nblocks) for pi in range(len(DIL_PATTERNS))])

    def hosted(long_matmul):
        ctx = dilated_scores(*next(groups))
        out = long_matmul()
        dilated_values(ctx)
        return out

    mx0 = hosted(lambda: scores(0, 0))
    for j in range(nblocks):
        p0 = numerators(0, mx0)
        mx1 = hosted(lambda: scores(j, 1))
        o0 = weighted_values(p0)
        p1 = numerators(1, mx1)
        if j + 1 < nblocks:
            mx0 = hosted(lambda: scores(j + 1, 0))
        o1 = hosted(lambda: weighted_values(p1))
        finish(j, o0, o1)

    chunk = 256

    def combine(c, carry):
        sl = pl.ds(pl.multiple_of(c * chunk, chunk), chunk)
        ms = [m_scr[pi, sl, :] for pi in range(len(DIL_PATTERNS))]
        top = functools.reduce(jnp.maximum, ms)
        num = jnp.zeros((chunk, LANES), F32)
        den = jnp.zeros((chunk, LANES), F32)
        for pi in range(len(DIL_PATTERNS)):
            a = jnp.exp2(ms[pi] - top)
            num = num + a * n_scr[pi, sl, :]
            den = den + a * z_scr[pi, sl, :]
        og_ref[0, sl, :] = (num / den).astype(og_ref.dtype)
        return carry

    lax.fori_loop(0, S // chunk, combine, 0)


def _attention(qt, kd, vt, dil_qkv, lq1, lk1, lq2, lk2, g_subln):
    B, S, _ = kd.shape
    nblk = D_DIL // LANES
    assert nblk == N_HEADS_DIFF
    small = lambda n: pl.BlockSpec((1, n), lambda b, h: (0, 0))
    biases = _dil_bias(S)
    const = lambda a: pl.BlockSpec(a.shape, lambda b, h: (0, 0, 0), pipeline_mode=pl.Buffered(1))
    s_scr = pltpu.VMEM((S, DIFF_Q_ROWS), F32)
    stat_scr = pltpu.VMEM((len(DIL_PATTERNS), S, LANES), F32)
    return pl.pallas_call(
        _attn_kernel,
        grid=(B, N_HEADS_DIFF),
        in_specs=[
            pl.BlockSpec((1, LANES, S), lambda b, h: (b, h, 0)),
            pl.BlockSpec((1, S, LANES), lambda b, h: (b, 0, h)),
            pl.BlockSpec((1, LANES, S), lambda b, h: (b, h, 0)),
            small(HEAD_DIM), small(HEAD_DIM), small(HEAD_DIM), small(HEAD_DIM),
            pl.BlockSpec((2 * HEAD_DIM, 1), lambda b, h: (0, 0)),
            pl.BlockSpec((1, S, LANES), lambda b, h: (b, 0, h)),
            pl.BlockSpec((1, S, LANES), lambda b, h: (b, 0, nblk + h)),
            pl.BlockSpec((1, S, LANES), lambda b, h: (b, 0, 2 * nblk + h)),
        ] + [const(a) for a in biases],
        out_specs=[
            pl.BlockSpec((1, S, LANES), lambda b, h: (b, 0, h)),
            pl.BlockSpec((1, S, LANES), lambda b, h: (b, 0, h)),
        ],
        out_shape=[
            jax.ShapeDtypeStruct((B, S, D_DIFF), BF16),
            jax.ShapeDtypeStruct((B, S, D_DIL), BF16),
        ],
        scratch_shapes=[s_scr, s_scr, stat_scr, stat_scr, stat_scr],
        compiler_params=pltpu.CompilerParams(
            dimension_semantics=("parallel", "parallel"),
            vmem_limit_bytes=VMEM_LIMIT_BYTES),
        name="attention",
    )(qt, kd, vt, lq1, lk1, lq2, lk2, g_subln.reshape(2 * HEAD_DIM, 1),
      dil_qkv, dil_qkv, dil_qkv, *biases)


def _post_kernel(x_ref, od_ref, og_ref, p_ref, wo_ref, gm_ref, wu_ref, wd_ref,
                 gp_ref, wg_ref, wp_ref, gf_ref, o_ref):
    half_rows = x_ref.shape[0] // POST_HALVES
    halves = [pl.ds(i * half_rows, half_rows) for i in range(POST_HALVES)]

    h = []
    for sl in halves:
        mixed = jnp.concatenate([od_ref[sl, :], og_ref[sl, :]], axis=-1)
        h.append(x_ref[sl, :] + jnp.dot(mixed, wo_ref[...], preferred_element_type=F32))

    for i in range(POST_HALVES):
        hn = _rms(h[i], gm_ref[...], NORM_EPS).astype(BF16)
        for c in range(D_FF // FF_CHUNK):
            u = jnp.dot(hn, wu_ref[:, c * FF_CHUNK:(c + 1) * FF_CHUNK], preferred_element_type=F32)
            u = jnp.square(jnp.maximum(u, 0.0)).astype(BF16)
            h[i] = h[i] + jnp.dot(u, wd_ref[c * FF_CHUNK:(c + 1) * FF_CHUNK, :],
                                  preferred_element_type=F32)

    for i, sl in enumerate(halves):
        hn = _rms(h[i], gp_ref[...], NORM_EPS).astype(BF16)
        gate = jax.nn.sigmoid(jnp.dot(hn, wg_ref[...], preferred_element_type=F32))
        emb = jnp.dot(p_ref[sl, :].astype(BF16), wp_ref[...], preferred_element_type=F32)
        o_ref[sl, :] = _rms(h[i] + gate * emb, gf_ref[...], NORM_EPS)


def _post(x2, od2, og2, p2, w_out, g_mlp, w_up, w_down, g_ple, w_gate, w_ple, g_final):
    N, D = x2.shape
    rows = POST_ROWS
    tok = lambda width: pl.BlockSpec((rows, width), lambda i: (i, 0))
    whole = lambda a: pl.BlockSpec(a.shape, lambda i: (0, 0), pipeline_mode=pl.Buffered(1))
    return pl.pallas_call(
        _post_kernel,
        grid=(N // rows,),
        in_specs=[
            tok(D), tok(D_DIFF), tok(D_DIL), tok(D_PLE),
            whole(w_out), whole(g_mlp), whole(w_up), whole(w_down),
            whole(g_ple), whole(w_gate), whole(w_ple), whole(g_final),
        ],
        out_specs=tok(D),
        out_shape=jax.ShapeDtypeStruct((N, D), F32),
        compiler_params=pltpu.CompilerParams(
            dimension_semantics=("parallel",),
            vmem_limit_bytes=VMEM_LIMIT_BYTES),
        name="out_mlp_ple",
    )(x2, od2, og2, p2, w_out, g_mlp, w_up, w_down, g_ple, w_gate, w_ple, g_final)


def _rope_tables(seq):
    inv = ROPE_THETA ** (-np.arange(0, HEAD_DIM, 2, dtype=np.float64) / HEAD_DIM)
    ang = np.arange(seq, dtype=np.float64)[:, None] * inv[None, :]
    cos, sin = np.cos(ang), np.sin(ang)
    reps = LANES // HEAD_DIM
    cos_t = np.tile(np.concatenate([cos, cos], axis=-1), (1, reps))
    sin_t = np.tile(np.concatenate([-sin, sin], axis=-1), (1, reps))
    return tuple(jnp.asarray(t, dtype=F32) for t in (cos_t, sin_t, cos.T, sin.T))


def kernel(x, p, w_in, w_out, g_mix, lambda_q1, lambda_k1, lambda_q2, lambda_k2, g_subln,
           g_mlp, w_up, w_down, g_ple, w_ple_gate, w_ple_proj, g_final):
    B, S, D = x.shape
    assert x.shape[1:] == (2048, D_MODEL) and w_in.shape == (1, D_MODEL, 2 * D_GROUP)

    later = (w_out[0], w_up[0], w_down[0], w_ple_gate[0], w_ple_proj[0])
    (qt, kd, vt, dil_qkv), (wo, wu, wd, wg, wp) = _project(x, g_mix, w_in[0], _rope_tables(S), later)
    od, og = _attention(qt, kd, vt, dil_qkv, lambda_q1, lambda_k1, lambda_q2, lambda_k2, g_subln)

    out = _post(
        x.reshape(B * S, D), od.reshape(B * S, D_DIFF), og.reshape(B * S, D_DIL),
        p[0].reshape(B * S, D_PLE),
        wo, g_mlp, wu, wd, g_ple, wg, wp, g_final.reshape(1, D))
    return out.reshape(B, S, D)
```

```python
import functools
import math

import jax
import jax.numpy as jnp
import numpy as np
from jax import lax
from jax.experimental import pallas as pl
from jax.experimental.pallas import tpu as pltpu

D_MODEL = 1024
HEAD_DIM = 64
D_DIFF = 512
D_DIL = 512
D_GROUP = 3 * D_DIFF
N_HEADS_DIFF = 4
DIL_PATTERNS = ((128, 1), (512, 4), (2048, 16))
D_FF = 4 * D_MODEL
D_PLE = 256
ROPE_THETA = 10000.0
NORM_EPS = 1e-6
SUBLN_EPS = 1e-5
NEG_INF = -1e30
LAM_INIT = 0.8 - 0.6 * math.exp(-0.3 * 0)
LOG2_E = math.log2(math.e)

LANES = 128
BF16_SUBLANES = 16
VMEM_LIMIT_BYTES = 56 * 1024 * 1024

PREP_ROWS = 256
PROJ_ROWS = 1024
PROJ_HALVES = 2
DIFF_Q_ROWS = 512
DIL_Q_ROWS = 128
POST_ROWS = 512
POST_HALVES = 2
FF_CHUNK = 1024

BF16 = jnp.bfloat16
F32 = jnp.float32
NT_DIMS = (((1,), (1,)), ((), ()))


def _rms(x, g, eps):
    return x * lax.rsqrt(jnp.mean(x * x, axis=-1, keepdims=True) + eps) * g


def _proj_kernel(n_cast, x_ref, g_ref, wqt_ref, wvt_ref, w_ref, cos_ref, sin_ref, cost_ref, sint_ref,
                 *rest):
    cast_in, (qt_ref, kd_ref, vt_ref, dil_ref), cast_out = rest[:n_cast], rest[n_cast:n_cast + 4], rest[n_cast + 4:]
    for src_ref, dst_ref in zip(cast_in, cast_out):
        dst_ref[...] = src_ref[...].astype(dst_ref.dtype)

    scale = HEAD_DIM ** -0.5 * LOG2_E
    half = HEAD_DIM // 2
    sub_rows = x_ref.shape[1] // PROJ_HALVES
    lane = lax.broadcasted_iota(jnp.int32, (sub_rows, LANES), 1)
    first_half = (lane % HEAD_DIM) < half
    slabs = ((kd_ref, 0, True, False), (dil_ref, 0, True, True),
             (dil_ref, D_DIL, True, False), (dil_ref, 2 * D_DIL, False, False))

    for i in range(PROJ_HALVES):
        sl = pl.ds(i * sub_rows, sub_rows)
        hn = _rms(x_ref[0, sl, :], g_ref[...], NORM_EPS).astype(BF16)
        cos = cos_ref[sl, :]
        sin = sin_ref[sl, :]

        def rope(t, cos=cos, sin=sin):
            swapped = jnp.where(first_half,
                                pltpu.roll(t, LANES - half, 1),
                                pltpu.roll(t, half, 1))
            return t * cos + swapped * sin

        qt = lax.dot_general(wqt_ref[...], hn, NT_DIMS, preferred_element_type=F32)
        cos_t = cost_ref[:, sl] * scale
        sin_t = sint_ref[:, sl] * scale
        for hb in range(D_DIFF // HEAD_DIM):
            t1 = qt[hb * HEAD_DIM:hb * HEAD_DIM + half]
            t2 = qt[hb * HEAD_DIM + half:(hb + 1) * HEAD_DIM]
            qt_ref[0, hb * HEAD_DIM:hb * HEAD_DIM + half, sl] = (t1 * cos_t - t2 * sin_t).astype(BF16)
            qt_ref[0, hb * HEAD_DIM + half:(hb + 1) * HEAD_DIM, sl] = (t1 * sin_t + t2 * cos_t).astype(BF16)

        vt_ref[0, :, sl] = lax.dot_general(wvt_ref[...], hn, NT_DIMS,
                                           preferred_element_type=F32).astype(BF16)

        for n, (out_ref, o0, use_rope, use_scale) in enumerate(slabs):
            acc = jnp.dot(hn, w_ref[:, n * D_DIFF:(n + 1) * D_DIFF], preferred_element_type=F32)
            for c in range(D_DIFF // LANES):
                t = acc[:, c * LANES:(c + 1) * LANES]
                if use_rope:
                    t = rope(t)
                if use_scale:
                    t = t * scale
                out_ref[0, sl, o0 + c * LANES:o0 + (c + 1) * LANES] = t.astype(out_ref.dtype)


def _prep_kernel(w_ref, wqt_ref, wvt_ref, wrest_ref):
    wqt_ref[...] = w_ref[:, :D_DIFF].T.astype(BF16)
    wvt_ref[...] = w_ref[:, 2 * D_DIFF:3 * D_DIFF].T.astype(BF16)
    wrest_ref[:, :D_DIFF] = w_ref[:, D_DIFF:2 * D_DIFF].astype(BF16)
    wrest_ref[:, D_DIFF:] = w_ref[:, D_GROUP:].astype(BF16)


def _prepare_w_in(w_in):
    D = w_in.shape[0]
    rows = PREP_ROWS
    return pl.pallas_call(
        _prep_kernel,
        grid=(D // rows,),
        in_specs=[pl.BlockSpec((rows, 2 * D_GROUP), lambda i: (i, 0))],
        out_specs=[
            pl.BlockSpec((D_DIFF, rows), lambda i: (0, i)),
            pl.BlockSpec((D_DIFF, rows), lambda i: (0, i)),
            pl.BlockSpec((rows, D_DIFF + D_GROUP), lambda i: (i, 0)),
        ],
        out_shape=[
            jax.ShapeDtypeStruct((D_DIFF, D), BF16),
            jax.ShapeDtypeStruct((D_DIFF, D), BF16),
            jax.ShapeDtypeStruct((D, D_DIFF + D_GROUP), BF16),
        ],
        compiler_params=pltpu.CompilerParams(dimension_semantics=("parallel",)),
        name="prep_w_in",
    )(w_in)


def _project(x, g_mix, w_in, tables, later_weights):
    B, S, D = x.shape
    rows = PROJ_ROWS
    steps = B * (S // rows)
    slab = lambda w: pl.BlockSpec((w.shape[0] // steps, w.shape[1]), lambda b, i: (b * (S // rows) + i, 0))
    cos_t, sin_t, cos_tt, sin_tt = tables
    wqt, wvt, w_rest = _prepare_w_in(w_in)
    const = lambda a: pl.BlockSpec(a.shape, lambda b, i: (0, 0), pipeline_mode=pl.Buffered(1))
    outs = pl.pallas_call(
        functools.partial(_proj_kernel, len(later_weights)),
        grid=(B, S // rows),
        in_specs=[
            pl.BlockSpec((1, rows, D), lambda b, i: (b, i, 0)),
            pl.BlockSpec((1, D), lambda b, i: (0, 0)),
            const(wqt), const(wvt), const(w_rest),
            pl.BlockSpec((rows, LANES), lambda b, i: (i, 0)),
            pl.BlockSpec((rows, LANES), lambda b, i: (i, 0)),
            pl.BlockSpec((HEAD_DIM // 2, rows), lambda b, i: (0, i)),
            pl.BlockSpec((HEAD_DIM // 2, rows), lambda b, i: (0, i)),
        ] + [slab(w) for w in later_weights],
        out_specs=[
            pl.BlockSpec((1, D_DIFF, rows), lambda b, i: (b, 0, i)),
            pl.BlockSpec((1, rows, D_DIFF), lambda b, i: (b, i, 0)),
            pl.BlockSpec((1, D_DIFF, rows), lambda b, i: (b, 0, i)),
            pl.BlockSpec((1, rows, D_GROUP), lambda b, i: (b, i, 0)),
        ] + [slab(w) for w in later_weights],
        out_shape=[
            jax.ShapeDtypeStruct((B, D_DIFF, S), BF16),
            jax.ShapeDtypeStruct((B, S, D_DIFF), BF16),
            jax.ShapeDtypeStruct((B, D_DIFF, S), BF16),
            jax.ShapeDtypeStruct((B, S, D_GROUP), F32),
        ] + [jax.ShapeDtypeStruct(w.shape, BF16) for w in later_weights],
        compiler_params=pltpu.CompilerParams(
            dimension_semantics=("parallel", "parallel"),
            vmem_limit_bytes=VMEM_LIMIT_BYTES),
        name="proj_rope",
    )(x, g_mix, wqt, wvt, w_rest, cos_t, sin_t, cos_tt, sin_tt, *later_weights)
    return outs[:4], outs[4:]


def _dil_geometry(window, d, seq):
    radius, L = window // (2 * d), seq // d
    wk = min(DIL_Q_ROWS + 2 * radius, L)
    return radius, L, wk, L // DIL_Q_ROWS


def _dil_bias(seq):
    out = []
    for window, d in DIL_PATTERNS:
        radius, L, wk, blocks = _dil_geometry(window, d, seq)
        row = np.arange(DIL_Q_ROWS)[:, None]
        col = np.arange(wk)[None, :]
        bias = []
        for t in range(d * blocks):
            q0 = (t % blocks) * DIL_Q_ROWS
            k0 = min(max(q0 - radius, 0), L - wk)
            bias.append(np.where(np.abs((k0 + col) - (q0 + row)) <= radius, 0.0, NEG_INF))
        out.append(jnp.asarray(np.stack(bias), dtype=F32))
    return out


def _attn_kernel(qt_ref, k_ref, vt_ref, lq1_ref, lk1_ref, lq2_ref, lk2_ref, g_ref,
                 dq_ref, dk_ref, dv_ref, b0_ref, b1_ref, b2_ref,
                 od_ref, og_ref, s0_scr, s1_scr, n_scr, m_scr, z_scr):
    S = k_ref.shape[1]
    rows = DIFF_Q_ROWS
    nblocks = S // rows
    k = k_ref[0]
    vt_ones = jnp.concatenate([vt_ref[0], jnp.ones((BF16_SUBLANES, S), BF16)], axis=0)
    lam = (jnp.exp(jnp.sum(lq1_ref[...] * lk1_ref[...], axis=-1, keepdims=True))
           - jnp.exp(jnp.sum(lq2_ref[...] * lk2_ref[...], axis=-1, keepdims=True))
           + LAM_INIT)
    gain = g_ref[...] * (1.0 - LAM_INIT)
    feat = lax.broadcasted_iota(jnp.int32, (2 * HEAD_DIM, rows), 0)
    s_scr = (s0_scr, s1_scr)

    def cols(j):
        return pl.ds(j * rows, rows)

    def scores(j, m):
        qt = qt_ref[0, :, cols(j)]
        in_map = (feat >= m * HEAD_DIM) & (feat < (m + 1) * HEAD_DIM)
        qm = jnp.where(in_map, qt, jnp.zeros_like(qt))
        s = jnp.dot(k, qm, preferred_element_type=F32)
        s_scr[m][...] = s
        return jnp.max(s, axis=0, keepdims=True)

    def numerators(m, mx):
        return jnp.exp2(s_scr[m][...] - mx).astype(BF16)

    def weighted_values(p):
        o = jnp.dot(vt_ones, p, preferred_element_type=F32)
        return o[:2 * HEAD_DIM] / o[2 * HEAD_DIM:2 * HEAD_DIM + 1]

    def finish(j, o0, o1):
        o = o0 - lam * o1
        o = o * lax.rsqrt(jnp.mean(o * o, axis=0, keepdims=True) + SUBLN_EPS) * gain
        od_ref[0, cols(j), :] = o.T.astype(od_ref.dtype)

    qrows = DIL_Q_ROWS
    head0 = lax.broadcasted_iota(jnp.int32, (1, 1, LANES), 2) < HEAD_DIM

    def dilated_scores(j, pi):
        (window, d), bias_ref = DIL_PATTERNS[pi], (b0_ref, b1_ref, b2_ref)[pi]
        radius, L, wk, blocks = _dil_geometry(window, d, S)
        per = d * blocks // nblocks
        qsl, ksl = [], []
        for i in range(per):
            t = j * per + i
            r, q0 = t // blocks, (t % blocks) * qrows
            k0 = min(max(q0 - radius, 0), L - wk)
            qsl.append(pl.ds(q0 * d + r, qrows, stride=d))
            ksl.append(pl.ds(k0 * d + r, wk, stride=d))
        qt = jnp.stack([dq_ref[0, sl, :].astype(BF16) for sl in qsl])
        kt = jnp.stack([dk_ref[0, sl, :].astype(BF16) for sl in ksl])
        vt = jnp.stack([dv_ref[0, sl, :].astype(BF16) for sl in ksl])
        bias = bias_ref[pl.ds(j * per, per)]

        def per_head(x):
            zero = jnp.zeros_like(x)
            return jnp.concatenate([jnp.where(head0, x, zero), jnp.where(head0, zero, x)], axis=1)

        s = jnp.einsum("tqc,tkc->tqk", qt, per_head(kt), preferred_element_type=F32)
        ps, mxs, zs = [], [], []
        for h in range(2):
            sh = s[:, :, h * wk:(h + 1) * wk] + bias
            mx = jnp.max(sh, axis=-1, keepdims=True)
            p = jnp.exp2(sh - mx)
            ps.append(p.astype(BF16))
            mxs.append(mx)
            zs.append(jnp.sum(p, axis=-1, keepdims=True))
        mx = jnp.where(head0, mxs[0], mxs[1])
        z = jnp.where(head0, zs[0], zs[1])
        return pi, qsl, per_head(vt), jnp.concatenate(ps, axis=-1), mx, z

    def dilated_values(ctx):
        pi, qsl, v2, p, mx, z = ctx
        n = jnp.einsum("tqk,tkc->tqc", p, v2, preferred_element_type=F32)
        for i, dst in enumerate(qsl):
            n_scr[pi, dst, :] = n[i]
            m_scr[pi, dst, :] = mx[i]
            z_scr[pi, dst, :] = z[i]

    groups = iter([(j, pi) for j in range(nblocks) for pi in range(len(DIL_PATTERNS))])

    def hosted(long_matmul):
        ctx = dilated_scores(*next(groups))
        out = long_matmul()
        dilated_values(ctx)
        return out

    mx0 = hosted(lambda: scores(0, 0))
    for j in range(nblocks):
        p0 = numerators(0, mx0)
        mx1 = hosted(lambda: scores(j, 1))
        o0 = weighted_values(p0)
        p1 = numerators(1, mx1)
        if j + 1 < nblocks:
            mx0 = hosted(lambda: scores(j + 1, 0))
        o1 = hosted(lambda: weighted_values(p1))
        finish(j, o0, o1)

    chunk = 256

    def combine(c, carry):
        sl = pl.ds(pl.multiple_of(c * chunk, chunk), chunk)
        ms = [m_scr[pi, sl, :] for pi in range(len(DIL_PATTERNS))]
        top = functools.reduce(jnp.maximum, ms)
        num = jnp.zeros((chunk, LANES), F32)
        den = jnp.zeros((chunk, LANES), F32)
        for pi in range(len(DIL_PATTERNS)):
            a = jnp.exp2(ms[pi] - top)
            num = num + a * n_scr[pi, sl, :]
            den = den + a * z_scr[pi, sl, :]
        og_ref[0, sl, :] = (num / den).astype(og_ref.dtype)
        return carry

    lax.fori_loop(0, S // chunk, combine, 0)


def _attention(qt, kd, vt, dil_qkv, lq1, lk1, lq2, lk2, g_subln):
    B, S, _ = kd.shape
    nblk = D_DIL // LANES
    assert nblk == N_HEADS_DIFF
    small = lambda n: pl.BlockSpec((1, n), lambda b, h: (0, 0))
    biases = _dil_bias(S)
    const = lambda a: pl.BlockSpec(a.shape, lambda b, h: (0, 0, 0), pipeline_mode=pl.Buffered(1))
    s_scr = pltpu.VMEM((S, DIFF_Q_ROWS), F32)
    stat_scr = pltpu.VMEM((len(DIL_PATTERNS), S, LANES), F32)
    return pl.pallas_call(
        _attn_kernel,
        grid=(B, N_HEADS_DIFF),
        in_specs=[
            pl.BlockSpec((1, LANES, S), lambda b, h: (b, h, 0)),
            pl.BlockSpec((1, S, LANES), lambda b, h: (b, 0, h)),
            pl.BlockSpec((1, LANES, S), lambda b, h: (b, h, 0)),
            small(HEAD_DIM), small(HEAD_DIM), small(HEAD_DIM), small(HEAD_DIM),
            pl.BlockSpec((2 * HEAD_DIM, 1), lambda b, h: (0, 0)),
            pl.BlockSpec((1, S, LANES), lambda b, h: (b, 0, h)),
            pl.BlockSpec((1, S, LANES), lambda b, h: (b, 0, nblk + h)),
            pl.BlockSpec((1, S, LANES), lambda b, h: (b, 0, 2 * nblk + h)),
        ] + [const(a) for a in biases],
        out_specs=[
            pl.BlockSpec((1, S, LANES), lambda b, h: (b, 0, h)),
            pl.BlockSpec((1, S, LANES), lambda b, h: (b, 0, h)),
        ],
        out_shape=[
            jax.ShapeDtypeStruct((B, S, D_DIFF), BF16),
            jax.ShapeDtypeStruct((B, S, D_DIL), BF16),
        ],
        scratch_shapes=[s_scr, s_scr, stat_scr, stat_scr, stat_scr],
        compiler_params=pltpu.CompilerParams(
            dimension_semantics=("parallel", "parallel"),
            vmem_limit_bytes=VMEM_LIMIT_BYTES),
        name="attention",
    )(qt, kd, vt, lq1, lk1, lq2, lk2, g_subln.reshape(2 * HEAD_DIM, 1),
      dil_qkv, dil_qkv, dil_qkv, *biases)


def _post_kernel(x_ref, od_ref, og_ref, p_ref, wo_ref, gm_ref, wu_ref, wd_ref,
                 gp_ref, wg_ref, wp_ref, gf_ref, o_ref):
    half_rows = x_ref.shape[0] // POST_HALVES
    halves = [pl.ds(i * half_rows, half_rows) for i in range(POST_HALVES)]

    h = []
    for sl in halves:
        mixed = jnp.concatenate([od_ref[sl, :], og_ref[sl, :]], axis=-1)
        h.append(x_ref[sl, :] + jnp.dot(mixed, wo_ref[...], preferred_element_type=F32))

    for i in range(POST_HALVES):
        hn = _rms(h[i], gm_ref[...], NORM_EPS).astype(BF16)
        for c in range(D_FF // FF_CHUNK):
            u = jnp.dot(hn, wu_ref[:, c * FF_CHUNK:(c + 1) * FF_CHUNK], preferred_element_type=F32)
            u = jnp.square(jnp.maximum(u, 0.0)).astype(BF16)
            h[i] = h[i] + jnp.dot(u, wd_ref[c * FF_CHUNK:(c + 1) * FF_CHUNK, :],
                                  preferred_element_type=F32)

    for i, sl in enumerate(halves):
        hn = _rms(h[i], gp_ref[...], NORM_EPS).astype(BF16)
        gate = jax.nn.sigmoid(jnp.dot(hn, wg_ref[...], preferred_element_type=F32))
        emb = jnp.dot(p_ref[sl, :].astype(BF16), wp_ref[...], preferred_element_type=F32)
        o_ref[sl, :] = _rms(h[i] + gate * emb, gf_ref[...], NORM_EPS)


def _post(x2, od2, og2, p2, w_out, g_mlp, w_up, w_down, g_ple, w_gate, w_ple, g_final):
    N, D = x2.shape
    rows = POST_ROWS
    tok = lambda width: pl.BlockSpec((rows, width), lambda i: (i, 0))
    whole = lambda a: pl.BlockSpec(a.shape, lambda i: (0, 0), pipeline_mode=pl.Buffered(1))
    return pl.pallas_call(
        _post_kernel,
        grid=(N // rows,),
        in_specs=[
            tok(D), tok(D_DIFF), tok(D_DIL), tok(D_PLE),
            whole(w_out), whole(g_mlp), whole(w_up), whole(w_down),
            whole(g_ple), whole(w_gate), whole(w_ple), whole(g_final),
        ],
        out_specs=tok(D),
        out_shape=jax.ShapeDtypeStruct((N, D), F32),
        compiler_params=pltpu.CompilerParams(
            dimension_semantics=("parallel",),
            vmem_limit_bytes=VMEM_LIMIT_BYTES),
        name="out_mlp_ple",
    )(x2, od2, og2, p2, w_out, g_mlp, w_up, w_down, g_ple, w_gate, w_ple, g_final)


def _rope_tables(seq):
    inv = ROPE_THETA ** (-np.arange(0, HEAD_DIM, 2, dtype=np.float64) / HEAD_DIM)
    ang = np.arange(seq, dtype=np.float64)[:, None] * inv[None, :]
    cos, sin = np.cos(ang), np.sin(ang)
    reps = LANES // HEAD_DIM
    cos_t = np.tile(np.concatenate([cos, cos], axis=-1), (1, reps))
    sin_t = np.tile(np.concatenate([-sin, sin], axis=-1), (1, reps))
    return tuple(jnp.asarray(t, dtype=F32) for t in (cos_t, sin_t, cos.T, sin.T))


def kernel(x, p, w_in, w_out, g_mix, lambda_q1, lambda_k1, lambda_q2, lambda_k2, g_subln,
           g_mlp, w_up, w_down, g_ple, w_ple_gate, w_ple_proj, g_final):
    B, S, D = x.shape
    assert x.shape[1:] == (2048, D_MODEL) and w_in.shape == (1, D_MODEL, 2 * D_GROUP)

    later = (w_out[0], w_up[0], w_down[0], w_ple_gate[0], w_ple_proj[0])
    (qt, kd, vt, dil_qkv), (wo, wu, wd, wg, wp) = _project(x, g_mix, w_in[0], _rope_tables(S), later)
    od, og = _attention(qt, kd, vt, dil_qkv, lambda_q1, lambda_k1, lambda_q2, lambda_k2, g_subln)

    out = _post(
        x.reshape(B * S, D), od.reshape(B * S, D_DIFF), og.reshape(B * S, D_DIL),
        p[0].reshape(B * S, D_PLE),
        wo, g_mlp, wu, wd, g_ple, wg, wp, g_final.reshape(1, D))
    return out.reshape(B, S, D)
```

```python
import functools
import math

import jax
import jax.numpy as jnp
import numpy as np
from jax import lax
from jax.experimental import pallas as pl
from jax.experimental.pallas import tpu as pltpu

D_MODEL = 1024
HEAD_DIM = 64
D_DIFF = 512
D_DIL = 512
D_GROUP = 3 * D_DIFF
N_HEADS_DIFF = 4
DIL_PATTERNS = ((128, 1), (512, 4), (2048, 16))
D_FF = 4 * D_MODEL
D_PLE = 256
ROPE_THETA = 10000.0
NORM_EPS = 1e-6
SUBLN_EPS = 1e-5
NEG_INF = -1e30
LAM_INIT = 0.8 - 0.6 * math.exp(-0.3 * 0)
LOG2_E = math.log2(math.e)

LANES = 128
BF16_SUBLANES = 16
VMEM_LIMIT_BYTES = 56 * 1024 * 1024

PREP_ROWS = 256
PROJ_ROWS = 1024
PROJ_HALVES = 2
DIFF_Q_ROWS = 512
DIL_Q_ROWS = 128
DIL_SPLIT = 4
POST_ROWS = 512
POST_HALVES = 2
FF_CHUNK = 1024

BF16 = jnp.bfloat16
F32 = jnp.float32
NT_DIMS = (((1,), (1,)), ((), ()))


def _rms(x, g, eps):
    return x * lax.rsqrt(jnp.mean(x * x, axis=-1, keepdims=True) + eps) * g


def _proj_kernel(n_cast, x_ref, g_ref, wqt_ref, wvt_ref, w_ref, cos_ref, sin_ref, cost_ref, sint_ref,
                 *rest):
    cast_in, (qt_ref, kd_ref, vt_ref, dil_ref), cast_out = rest[:n_cast], rest[n_cast:n_cast + 4], rest[n_cast + 4:]
    for src_ref, dst_ref in zip(cast_in, cast_out):
        dst_ref[...] = src_ref[...].astype(dst_ref.dtype)

    scale = HEAD_DIM ** -0.5 * LOG2_E
    half = HEAD_DIM // 2
    sub_rows = x_ref.shape[1] // PROJ_HALVES
    lane = lax.broadcasted_iota(jnp.int32, (sub_rows, LANES), 1)
    first_half = (lane % HEAD_DIM) < half
    slabs = ((kd_ref, 0, True, False), (dil_ref, 0, True, True),
             (dil_ref, D_DIL, True, False), (dil_ref, 2 * D_DIL, False, False))

    for i in range(PROJ_HALVES):
        sl = pl.ds(i * sub_rows, sub_rows)
        hn = _rms(x_ref[0, sl, :], g_ref[...], NORM_EPS).astype(BF16)
        cos = cos_ref[sl, :]
        sin = sin_ref[sl, :]

        def rope(t, cos=cos, sin=sin):
            swapped = jnp.where(first_half,
                                pltpu.roll(t, LANES - half, 1),
                                pltpu.roll(t, half, 1))
            return t * cos + swapped * sin

        qt = lax.dot_general(wqt_ref[...], hn, NT_DIMS, preferred_element_type=F32)
        cos_t = cost_ref[:, sl] * scale
        sin_t = sint_ref[:, sl] * scale
        for hb in range(D_DIFF // HEAD_DIM):
            t1 = qt[hb * HEAD_DIM:hb * HEAD_DIM + half]
            t2 = qt[hb * HEAD_DIM + half:(hb + 1) * HEAD_DIM]
            qt_ref[0, hb * HEAD_DIM:hb * HEAD_DIM + half, sl] = (t1 * cos_t - t2 * sin_t).astype(BF16)
            qt_ref[0, hb * HEAD_DIM + half:(hb + 1) * HEAD_DIM, sl] = (t1 * sin_t + t2 * cos_t).astype(BF16)

        vt_ref[0, :, sl] = lax.dot_general(wvt_ref[...], hn, NT_DIMS,
                                           preferred_element_type=F32).astype(BF16)

        for n, (out_ref, o0, use_rope, use_scale) in enumerate(slabs):
            acc = jnp.dot(hn, w_ref[:, n * D_DIFF:(n + 1) * D_DIFF], preferred_element_type=F32)
            for c in range(D_DIFF // LANES):
                t = acc[:, c * LANES:(c + 1) * LANES]
                if use_rope:
                    t = rope(t)
                if use_scale:
                    t = t * scale
                out_ref[0, sl, o0 + c * LANES:o0 + (c + 1) * LANES] = t.astype(out_ref.dtype)


def _prep_kernel(w_ref, wqt_ref, wvt_ref, wrest_ref):
    wqt_ref[...] = w_ref[:, :D_DIFF].T.astype(BF16)
    wvt_ref[...] = w_ref[:, 2 * D_DIFF:3 * D_DIFF].T.astype(BF16)
    wrest_ref[:, :D_DIFF] = w_ref[:, D_DIFF:2 * D_DIFF].astype(BF16)
    wrest_ref[:, D_DIFF:] = w_ref[:, D_GROUP:].astype(BF16)


def _prepare_w_in(w_in):
    D = w_in.shape[0]
    rows = PREP_ROWS
    return pl.pallas_call(
        _prep_kernel,
        grid=(D // rows,),
        in_specs=[pl.BlockSpec((rows, 2 * D_GROUP), lambda i: (i, 0))],
        out_specs=[
            pl.BlockSpec((D_DIFF, rows), lambda i: (0, i)),
            pl.BlockSpec((D_DIFF, rows), lambda i: (0, i)),
            pl.BlockSpec((rows, D_DIFF + D_GROUP), lambda i: (i, 0)),
        ],
        out_shape=[
            jax.ShapeDtypeStruct((D_DIFF, D), BF16),
            jax.ShapeDtypeStruct((D_DIFF, D), BF16),
            jax.ShapeDtypeStruct((D, D_DIFF + D_GROUP), BF16),
        ],
        compiler_params=pltpu.CompilerParams(dimension_semantics=("parallel",)),
        name="prep_w_in",
    )(w_in)


def _project(x, g_mix, w_in, tables, later_weights):
    B, S, D = x.shape
    rows = PROJ_ROWS
    steps = B * (S // rows)
    slab = lambda w: pl.BlockSpec((w.shape[0] // steps, w.shape[1]), lambda b, i: (b * (S // rows) + i, 0))
    cos_t, sin_t, cos_tt, sin_tt = tables
    wqt, wvt, w_rest = _prepare_w_in(w_in)
    const = lambda a: pl.BlockSpec(a.shape, lambda b, i: (0, 0), pipeline_mode=pl.Buffered(1))
    outs = pl.pallas_call(
        functools.partial(_proj_kernel, len(later_weights)),
        grid=(B, S // rows),
        in_specs=[
            pl.BlockSpec((1, rows, D), lambda b, i: (b, i, 0)),
            pl.BlockSpec((1, D), lambda b, i: (0, 0)),
            const(wqt), const(wvt), const(w_rest),
            pl.BlockSpec((rows, LANES), lambda b, i: (i, 0)),
            pl.BlockSpec((rows, LANES), lambda b, i: (i, 0)),
            pl.BlockSpec((HEAD_DIM // 2, rows), lambda b, i: (0, i)),
            pl.BlockSpec((HEAD_DIM // 2, rows), lambda b, i: (0, i)),
        ] + [slab(w) for w in later_weights],
        out_specs=[
            pl.BlockSpec((1, D_DIFF, rows), lambda b, i: (b, 0, i)),
            pl.BlockSpec((1, rows, D_DIFF), lambda b, i: (b, i, 0)),
            pl.BlockSpec((1, D_DIFF, rows), lambda b, i: (b, 0, i)),
            pl.BlockSpec((1, rows, D_GROUP), lambda b, i: (b, i, 0)),
        ] + [slab(w) for w in later_weights],
        out_shape=[
            jax.ShapeDtypeStruct((B, D_DIFF, S), BF16),
            jax.ShapeDtypeStruct((B, S, D_DIFF), BF16),
            jax.ShapeDtypeStruct((B, D_DIFF, S), BF16),
            jax.ShapeDtypeStruct((B, S, D_GROUP), F32),
        ] + [jax.ShapeDtypeStruct(w.shape, BF16) for w in later_weights],
        compiler_params=pltpu.CompilerParams(
            dimension_semantics=("parallel", "parallel"),
            vmem_limit_bytes=VMEM_LIMIT_BYTES),
        name="proj_rope",
    )(x, g_mix, wqt, wvt, w_rest, cos_t, sin_t, cos_tt, sin_tt, *later_weights)
    return outs[:4], outs[4:]


def _dil_geometry(window, d, seq):
    radius, L = window // (2 * d), seq // d
    wk = min(DIL_Q_ROWS + 2 * radius, L)
    return radius, L, wk, L // DIL_Q_ROWS


def _dil_bias(seq):
    out = []
    for window, d in DIL_PATTERNS:
        radius, L, wk, blocks = _dil_geometry(window, d, seq)
        row = np.arange(DIL_Q_ROWS)[:, None]
        col = np.arange(wk)[None, :]
        bias = []
        for t in range(d * blocks):
            q0 = (t % blocks) * DIL_Q_ROWS
            k0 = min(max(q0 - radius, 0), L - wk)
            bias.append(np.where(np.abs((k0 + col) - (q0 + row)) <= radius, 0.0, NEG_INF))
        out.append(jnp.asarray(np.stack(bias), dtype=F32))
    return out


def _attn_kernel(qt_ref, k_ref, vt_ref, lq1_ref, lk1_ref, lq2_ref, lk2_ref, g_ref,
                 dq_ref, dk_ref, dv_ref, b0_ref, b1_ref, b2_ref,
                 od_ref, og_ref, s0_scr, s1_scr, n_scr, m_scr, z_scr, q4_scr, k4_scr, v4_scr):
    S = k_ref.shape[1]
    rows = DIFF_Q_ROWS
    nblocks = S // rows
    k = k_ref[0]
    vt_ones = jnp.concatenate([vt_ref[0], jnp.ones((BF16_SUBLANES, S), BF16)], axis=0)
    lam = (jnp.exp(jnp.sum(lq1_ref[...] * lk1_ref[...], axis=-1, keepdims=True))
           - jnp.exp(jnp.sum(lq2_ref[...] * lk2_ref[...], axis=-1, keepdims=True))
           + LAM_INIT)
    gain = g_ref[...] * (1.0 - LAM_INIT)
    feat = lax.broadcasted_iota(jnp.int32, (2 * HEAD_DIM, rows), 0)
    s_scr = (s0_scr, s1_scr)

    def cols(j):
        return pl.ds(j * rows, rows)

    def scores(j, m):
        qt = qt_ref[0, :, cols(j)]
        in_map = (feat >= m * HEAD_DIM) & (feat < (m + 1) * HEAD_DIM)
        qm = jnp.where(in_map, qt, jnp.zeros_like(qt))
        s = jnp.dot(k, qm, preferred_element_type=F32)
        s_scr[m][...] = s
        return jnp.max(s, axis=0, keepdims=True)

    def numerators(m, mx):
        return jnp.exp2(s_scr[m][...] - mx).astype(BF16)

    def weighted_values(p):
        o = jnp.dot(vt_ones, p, preferred_element_type=F32)
        return o[:2 * HEAD_DIM] / o[2 * HEAD_DIM:2 * HEAD_DIM + 1]

    def finish(j, o0, o1):
        o = o0 - lam * o1
        o = o * lax.rsqrt(jnp.mean(o * o, axis=0, keepdims=True) + SUBLN_EPS) * gain
        od_ref[0, cols(j), :] = o.T.astype(od_ref.dtype)

    qrows = DIL_Q_ROWS
    head0 = lax.broadcasted_iota(jnp.int32, (1, 1, LANES), 2) < HEAD_DIM

    quarter = S // DIL_SPLIT
    for nat_ref, split_scr in ((dq_ref, q4_scr), (dk_ref, k4_scr), (dv_ref, v4_scr)):
        for a in range(DIL_SPLIT):
            split_scr[pl.ds(a * quarter, quarter), :] = nat_ref[0, pl.ds(a, quarter, stride=DIL_SPLIT), :]

    def class_rows(nat_ref, split_scr, d, r, start, size):
        if d == 1:
            return nat_ref[0, pl.ds(start, size), :]
        step = d // DIL_SPLIT
        first = (r % DIL_SPLIT) * quarter + start * step + r // DIL_SPLIT
        return split_scr[pl.ds(first, size, stride=step), :]

    def dilated_scores(j, pi):
        (window, d), bias_ref = DIL_PATTERNS[pi], (b0_ref, b1_ref, b2_ref)[pi]
        radius, L, wk, blocks = _dil_geometry(window, d, S)
        per = d * blocks // nblocks
        qsl, qs, ks, vs = [], [], [], []
        for i in range(per):
            t = j * per + i
            r, q0 = t // blocks, (t % blocks) * qrows
            k0 = min(max(q0 - radius, 0), L - wk)
            qsl.append(pl.ds(q0 * d + r, qrows, stride=d))
            qs.append(class_rows(dq_ref, q4_scr, d, r, q0, qrows))
            ks.append(class_rows(dk_ref, k4_scr, d, r, k0, wk))
            vs.append(class_rows(dv_ref, v4_scr, d, r, k0, wk))
        qt = jnp.stack(qs).astype(BF16)
        kt = jnp.stack(ks).astype(BF16)
        vt = jnp.stack(vs).astype(BF16)
        bias = bias_ref[pl.ds(j * per, per)]

        def per_head(x):
            zero = jnp.zeros_like(x)
            return jnp.concatenate([jnp.where(head0, x, zero), jnp.where(head0, zero, x)], axis=1)

        s = jnp.einsum("tqc,tkc->tqk", qt, per_head(kt), preferred_element_type=F32)
        ps, mxs, zs = [], [], []
        for h in range(2):
            sh = s[:, :, h * wk:(h + 1) * wk] + bias
            mx = jnp.max(sh, axis=-1, keepdims=True)
            p = jnp.exp2(sh - mx)
            ps.append(p.astype(BF16))
            mxs.append(mx)
            zs.append(jnp.sum(p, axis=-1, keepdims=True))
        mx = jnp.where(head0, mxs[0], mxs[1])
        z = jnp.where(head0, zs[0], zs[1])
        return pi, qsl, per_head(vt), jnp.concatenate(ps, axis=-1), mx, z

    def dilated_values(ctx):
        pi, qsl, v2, p, mx, z = ctx
        n = jnp.einsum("tqk,tkc->tqc", p, v2, preferred_element_type=F32)
        for i, dst in enumerate(qsl):
            n_scr[pi, dst, :] = n[i]
            m_scr[pi, dst, :] = mx[i]
            z_scr[pi, dst, :] = z[i]

    groups = iter([(j, pi) for j in range(nblocks) for pi in range(len(DIL_PATTERNS))])

    def hosted(long_matmul):
        ctx = dilated_scores(*next(groups))
        out = long_matmul()
        dilated_values(ctx)
        return out

    mx0 = hosted(lambda: scores(0, 0))
    for j in range(nblocks):
        p0 = numerators(0, mx0)
        mx1 = hosted(lambda: scores(j, 1))
        o0 = weighted_values(p0)
        p1 = numerators(1, mx1)
        if j + 1 < nblocks:
            mx0 = hosted(lambda: scores(j + 1, 0))
        o1 = hosted(lambda: weighted_values(p1))
        finish(j, o0, o1)

    chunk = 256

    def combine(c, carry):
        sl = pl.ds(pl.multiple_of(c * chunk, chunk), chunk)
        ms = [m_scr[pi, sl, :] for pi in range(len(DIL_PATTERNS))]
        top = functools.reduce(jnp.maximum, ms)
        num = jnp.zeros((chunk, LANES), F32)
        den = jnp.zeros((chunk, LANES), F32)
        for pi in range(len(DIL_PATTERNS)):
            a = jnp.exp2(ms[pi] - top)
            num = num + a * n_scr[pi, sl, :]
            den = den + a * z_scr[pi, sl, :]
        og_ref[0, sl, :] = (num / den).astype(og_ref.dtype)
        return carry

    lax.fori_loop(0, S // chunk, combine, 0)


def _attention(qt, kd, vt, dil_qkv, lq1, lk1, lq2, lk2, g_subln):
    B, S, _ = kd.shape
    nblk = D_DIL // LANES
    assert nblk == N_HEADS_DIFF
    small = lambda n: pl.BlockSpec((1, n), lambda b, h: (0, 0))
    biases = _dil_bias(S)
    const = lambda a: pl.BlockSpec(a.shape, lambda b, h: (0, 0, 0), pipeline_mode=pl.Buffered(1))
    s_scr = pltpu.VMEM((S, DIFF_Q_ROWS), F32)
    stat_scr = pltpu.VMEM((len(DIL_PATTERNS), S, LANES), F32)
    return pl.pallas_call(
        _attn_kernel,
        grid=(B, N_HEADS_DIFF),
        in_specs=[
            pl.BlockSpec((1, LANES, S), lambda b, h: (b, h, 0)),
            pl.BlockSpec((1, S, LANES), lambda b, h: (b, 0, h)),
            pl.BlockSpec((1, LANES, S), lambda b, h: (b, h, 0)),
            small(HEAD_DIM), small(HEAD_DIM), small(HEAD_DIM), small(HEAD_DIM),
            pl.BlockSpec((2 * HEAD_DIM, 1), lambda b, h: (0, 0)),
            pl.BlockSpec((1, S, LANES), lambda b, h: (b, 0, h)),
            pl.BlockSpec((1, S, LANES), lambda b, h: (b, 0, nblk + h)),
            pl.BlockSpec((1, S, LANES), lambda b, h: (b, 0, 2 * nblk + h)),
        ] + [const(a) for a in biases],
        out_specs=[
            pl.BlockSpec((1, S, LANES), lambda b, h: (b, 0, h)),
            pl.BlockSpec((1, S, LANES), lambda b, h: (b, 0, h)),
        ],
        out_shape=[
            jax.ShapeDtypeStruct((B, S, D_DIFF), BF16),
            jax.ShapeDtypeStruct((B, S, D_DIL), BF16),
        ],
        scratch_shapes=[s_scr, s_scr, stat_scr, stat_scr, stat_scr] + [pltpu.VMEM((S, LANES), F32)] * 3,
        compiler_params=pltpu.CompilerParams(
            dimension_semantics=("parallel", "parallel"),
            vmem_limit_bytes=VMEM_LIMIT_BYTES),
        name="attention",
    )(qt, kd, vt, lq1, lk1, lq2, lk2, g_subln.reshape(2 * HEAD_DIM, 1),
      dil_qkv, dil_qkv, dil_qkv, *biases)


def _post_kernel(x_ref, od_ref, og_ref, p_ref, wo_ref, gm_ref, wu_ref, wd_ref,
                 gp_ref, wg_ref, wp_ref, gf_ref, o_ref):
    half_rows = x_ref.shape[0] // POST_HALVES
    halves = [pl.ds(i * half_rows, half_rows) for i in range(POST_HALVES)]

    h = []
    for sl in halves:
        mixed = jnp.concatenate([od_ref[sl, :], og_ref[sl, :]], axis=-1)
        h.append(x_ref[sl, :] + jnp.dot(mixed, wo_ref[...], preferred_element_type=F32))

    for i in range(POST_HALVES):
        hn = _rms(h[i], gm_ref[...], NORM_EPS).astype(BF16)
        for c in range(D_FF // FF_CHUNK):
            u = jnp.dot(hn, wu_ref[:, c * FF_CHUNK:(c + 1) * FF_CHUNK], preferred_element_type=F32)
            u = jnp.square(jnp.maximum(u, 0.0)).astype(BF16)
            h[i] = h[i] + jnp.dot(u, wd_ref[c * FF_CHUNK:(c + 1) * FF_CHUNK, :],
                                  preferred_element_type=F32)

    for i, sl in enumerate(halves):
        hn = _rms(h[i], gp_ref[...], NORM_EPS).astype(BF16)
        gate = jax.nn.sigmoid(jnp.dot(hn, wg_ref[...], preferred_element_type=F32))
        emb = jnp.dot(p_ref[sl, :].astype(BF16), wp_ref[...], preferred_element_type=F32)
        o_ref[sl, :] = _rms(h[i] + gate * emb, gf_ref[...], NORM_EPS)


def _post(x2, od2, og2, p2, w_out, g_mlp, w_up, w_down, g_ple, w_gate, w_ple, g_final):
    N, D = x2.shape
    rows = POST_ROWS
    tok = lambda width: pl.BlockSpec((rows, width), lambda i: (i, 0))
    whole = lambda a: pl.BlockSpec(a.shape, lambda i: (0, 0), pipeline_mode=pl.Buffered(1))
    return pl.pallas_call(
        _post_kernel,
        grid=(N // rows,),
        in_specs=[
            tok(D), tok(D_DIFF), tok(D_DIL), tok(D_PLE),
            whole(w_out), whole(g_mlp), whole(w_up), whole(w_down),
            whole(g_ple), whole(w_gate), whole(w_ple), whole(g_final),
        ],
        out_specs=tok(D),
        out_shape=jax.ShapeDtypeStruct((N, D), F32),
        compiler_params=pltpu.CompilerParams(
            dimension_semantics=("parallel",),
            vmem_limit_bytes=VMEM_LIMIT_BYTES),
        name="out_mlp_ple",
    )(x2, od2, og2, p2, w_out, g_mlp, w_up, w_down, g_ple, w_gate, w_ple, g_final)


def _rope_tables(seq):
    inv = ROPE_THETA ** (-np.arange(0, HEAD_DIM, 2, dtype=np.float64) / HEAD_DIM)
    ang = np.arange(seq, dtype=np.float64)[:, None] * inv[None, :]
    cos, sin = np.cos(ang), np.sin(ang)
    reps = LANES // HEAD_DIM
    cos_t = np.tile(np.concatenate([cos, cos], axis=-1), (1, reps))
    sin_t = np.tile(np.concatenate([-sin, sin], axis=-1), (1, reps))
    return tuple(jnp.asarray(t, dtype=F32) for t in (cos_t, sin_t, cos.T, sin.T))


def kernel(x, p, w_in, w_out, g_mix, lambda_q1, lambda_k1, lambda_q2, lambda_k2, g_subln,
           g_mlp, w_up, w_down, g_ple, w_ple_gate, w_ple_proj, g_final):
    B, S, D = x.shape
    assert x.shape[1:] == (2048, D_MODEL) and w_in.shape == (1, D_MODEL, 2 * D_GROUP)

    later = (w_out[0], w_up[0], w_down[0], w_ple_gate[0], w_ple_proj[0])
    (qt, kd, vt, dil_qkv), (wo, wu, wd, wg, wp) = _project(x, g_mix, w_in[0], _rope_tables(S), later)
    od, og = _attention(qt, kd, vt, dil_qkv, lambda_q1, lambda_k1, lambda_q2, lambda_k2, g_subln)

    out = _post(
        x.reshape(B * S, D), od.reshape(B * S, D_DIFF), og.reshape(B * S, D_DIL),
        p[0].reshape(B * S, D_PLE),
        wo, g_mlp, wu, wd, g_ple, wg, wp, g_final.reshape(1, D))
    return out.reshape(B, S, D)
```

```python
import functools
import math

import jax
import jax.numpy as jnp
import numpy as np
from jax import lax
from jax.experimental import pallas as pl
from jax.experimental.pallas import tpu as pltpu

D_MODEL = 1024
HEAD_DIM = 64
D_DIFF = 512
D_DIL = 512
D_GROUP = 3 * D_DIFF
N_HEADS_DIFF = 4
DIL_PATTERNS = ((128, 1), (512, 4), (2048, 16))
D_FF = 4 * D_MODEL
D_PLE = 256
ROPE_THETA = 10000.0
NORM_EPS = 1e-6
SUBLN_EPS = 1e-5
NEG_INF = -1e30
LAM_INIT = 0.8 - 0.6 * math.exp(-0.3 * 0)
LOG2_E = math.log2(math.e)

LANES = 128
BF16_SUBLANES = 16
VMEM_LIMIT_BYTES = 56 * 1024 * 1024

PREP_ROWS = 256
PROJ_ROWS = 1024
PROJ_HALVES = 2
DIFF_Q_ROWS = 512
DIL_Q_ROWS = 128
PAD_EVERY = 16
POST_ROWS = 512
POST_HALVES = 2
FF_CHUNK = 1024

BF16 = jnp.bfloat16
F32 = jnp.float32
NT_DIMS = (((1,), (1,)), ((), ()))


def _rms(x, g, eps):
    return x * lax.rsqrt(jnp.mean(x * x, axis=-1, keepdims=True) + eps) * g


def _proj_kernel(n_cast, x_ref, g_ref, wqt_ref, wvt_ref, w_ref, cos_ref, sin_ref, cost_ref, sint_ref,
                 *rest):
    cast_in, (qt_ref, kd_ref, vt_ref, dil_ref), cast_out = rest[:n_cast], rest[n_cast:n_cast + 4], rest[n_cast + 4:]
    for src_ref, dst_ref in zip(cast_in, cast_out):
        dst_ref[...] = src_ref[...].astype(dst_ref.dtype)

    scale = HEAD_DIM ** -0.5 * LOG2_E
    half = HEAD_DIM // 2
    sub_rows = x_ref.shape[1] // PROJ_HALVES
    lane = lax.broadcasted_iota(jnp.int32, (sub_rows, LANES), 1)
    first_half = (lane % HEAD_DIM) < half
    slabs = ((kd_ref, 0, True, False), (dil_ref, 0, True, True),
             (dil_ref, D_DIL, True, False), (dil_ref, 2 * D_DIL, False, False))

    for i in range(PROJ_HALVES):
        sl = pl.ds(i * sub_rows, sub_rows)
        hn = _rms(x_ref[0, sl, :], g_ref[...], NORM_EPS).astype(BF16)
        cos = cos_ref[sl, :]
        sin = sin_ref[sl, :]

        def rope(t, cos=cos, sin=sin):
            swapped = jnp.where(first_half,
                                pltpu.roll(t, LANES - half, 1),
                                pltpu.roll(t, half, 1))
            return t * cos + swapped * sin

        qt = lax.dot_general(wqt_ref[...], hn, NT_DIMS, preferred_element_type=F32)
        cos_t = cost_ref[:, sl] * scale
        sin_t = sint_ref[:, sl] * scale
        for hb in range(D_DIFF // HEAD_DIM):
            t1 = qt[hb * HEAD_DIM:hb * HEAD_DIM + half]
            t2 = qt[hb * HEAD_DIM + half:(hb + 1) * HEAD_DIM]
            qt_ref[0, hb * HEAD_DIM:hb * HEAD_DIM + half, sl] = (t1 * cos_t - t2 * sin_t).astype(BF16)
            qt_ref[0, hb * HEAD_DIM + half:(hb + 1) * HEAD_DIM, sl] = (t1 * sin_t + t2 * cos_t).astype(BF16)

        vt_ref[0, :, sl] = lax.dot_general(wvt_ref[...], hn, NT_DIMS,
                                           preferred_element_type=F32).astype(BF16)

        for n, (out_ref, o0, use_rope, use_scale) in enumerate(slabs):
            acc = jnp.dot(hn, w_ref[:, n * D_DIFF:(n + 1) * D_DIFF], preferred_element_type=F32)
            for c in range(D_DIFF // LANES):
                t = acc[:, c * LANES:(c + 1) * LANES]
                if use_rope:
                    t = rope(t)
                if use_scale:
                    t = t * scale
                out_ref[0, sl, o0 + c * LANES:o0 + (c + 1) * LANES] = t.astype(out_ref.dtype)


def _prep_kernel(w_ref, wqt_ref, wvt_ref, wrest_ref):
    wqt_ref[...] = w_ref[:, :D_DIFF].T.astype(BF16)
    wvt_ref[...] = w_ref[:, 2 * D_DIFF:3 * D_DIFF].T.astype(BF16)
    wrest_ref[:, :D_DIFF] = w_ref[:, D_DIFF:2 * D_DIFF].astype(BF16)
    wrest_ref[:, D_DIFF:] = w_ref[:, D_GROUP:].astype(BF16)


def _prepare_w_in(w_in):
    D = w_in.shape[0]
    rows = PREP_ROWS
    return pl.pallas_call(
        _prep_kernel,
        grid=(D // rows,),
        in_specs=[pl.BlockSpec((rows, 2 * D_GROUP), lambda i: (i, 0))],
        out_specs=[
            pl.BlockSpec((D_DIFF, rows), lambda i: (0, i)),
            pl.BlockSpec((D_DIFF, rows), lambda i: (0, i)),
            pl.BlockSpec((rows, D_DIFF + D_GROUP), lambda i: (i, 0)),
        ],
        out_shape=[
            jax.ShapeDtypeStruct((D_DIFF, D), BF16),
            jax.ShapeDtypeStruct((D_DIFF, D), BF16),
            jax.ShapeDtypeStruct((D, D_DIFF + D_GROUP), BF16),
        ],
        compiler_params=pltpu.CompilerParams(dimension_semantics=("parallel",)),
        name="prep_w_in",
    )(w_in)


def _project(x, g_mix, w_in, tables, later_weights):
    B, S, D = x.shape
    rows = PROJ_ROWS
    steps = B * (S // rows)
    slab = lambda w: pl.BlockSpec((w.shape[0] // steps, w.shape[1]), lambda b, i: (b * (S // rows) + i, 0))
    cos_t, sin_t, cos_tt, sin_tt = tables
    wqt, wvt, w_rest = _prepare_w_in(w_in)
    const = lambda a: pl.BlockSpec(a.shape, lambda b, i: (0, 0), pipeline_mode=pl.Buffered(1))
    outs = pl.pallas_call(
        functools.partial(_proj_kernel, len(later_weights)),
        grid=(B, S // rows),
        in_specs=[
            pl.BlockSpec((1, rows, D), lambda b, i: (b, i, 0)),
            pl.BlockSpec((1, D), lambda b, i: (0, 0)),
            const(wqt), const(wvt), const(w_rest),
            pl.BlockSpec((rows, LANES), lambda b, i: (i, 0)),
            pl.BlockSpec((rows, LANES), lambda b, i: (i, 0)),
            pl.BlockSpec((HEAD_DIM // 2, rows), lambda b, i: (0, i)),
            pl.BlockSpec((HEAD_DIM // 2, rows), lambda b, i: (0, i)),
        ] + [slab(w) for w in later_weights],
        out_specs=[
            pl.BlockSpec((1, D_DIFF, rows), lambda b, i: (b, 0, i)),
            pl.BlockSpec((1, rows, D_DIFF), lambda b, i: (b, i, 0)),
            pl.BlockSpec((1, D_DIFF, rows), lambda b, i: (b, 0, i)),
            pl.BlockSpec((1, rows, D_GROUP), lambda b, i: (b, i, 0)),
        ] + [slab(w) for w in later_weights],
        out_shape=[
            jax.ShapeDtypeStruct((B, D_DIFF, S), BF16),
            jax.ShapeDtypeStruct((B, S, D_DIFF), BF16),
            jax.ShapeDtypeStruct((B, D_DIFF, S), BF16),
            jax.ShapeDtypeStruct((B, S, D_GROUP), F32),
        ] + [jax.ShapeDtypeStruct(w.shape, BF16) for w in later_weights],
        compiler_params=pltpu.CompilerParams(
            dimension_semantics=("parallel", "parallel"),
            vmem_limit_bytes=VMEM_LIMIT_BYTES),
        name="proj_rope",
    )(x, g_mix, wqt, wvt, w_rest, cos_t, sin_t, cos_tt, sin_tt, *later_weights)
    return outs[:4], outs[4:]


def _dil_geometry(window, d, seq):
    radius, L = window // (2 * d), seq // d
    wk = min(DIL_Q_ROWS + 2 * radius, L)
    return radius, L, wk, L // DIL_Q_ROWS


def _dil_bias(seq):
    out = []
    for window, d in DIL_PATTERNS:
        radius, L, wk, blocks = _dil_geometry(window, d, seq)
        row = np.arange(DIL_Q_ROWS)[:, None]
        col = np.arange(wk)[None, :]
        bias = []
        for t in range(d * blocks):
            q0 = (t % blocks) * DIL_Q_ROWS
            k0 = min(max(q0 - radius, 0), L - wk)
            bias.append(np.where(np.abs((k0 + col) - (q0 + row)) <= radius, 0.0, NEG_INF))
        out.append(jnp.asarray(np.stack(bias), dtype=F32))
    return out


def _attn_kernel(qt_ref, k_ref, vt_ref, lq1_ref, lk1_ref, lq2_ref, lk2_ref, g_ref,
                 dq_ref, dk_ref, dv_ref, b0_ref, b1_ref, b2_ref,
                 od_ref, og_ref, s0_scr, s1_scr, n_scr, m_scr, z_scr, wide_scr):
    S = k_ref.shape[1]
    rows = DIFF_Q_ROWS
    nblocks = S // rows
    k = k_ref[0]
    vt_ones = jnp.concatenate([vt_ref[0], jnp.ones((BF16_SUBLANES, S), BF16)], axis=0)
    lam = (jnp.exp(jnp.sum(lq1_ref[...] * lk1_ref[...], axis=-1, keepdims=True))
           - jnp.exp(jnp.sum(lq2_ref[...] * lk2_ref[...], axis=-1, keepdims=True))
           + LAM_INIT)
    gain = g_ref[...] * (1.0 - LAM_INIT)
    feat = lax.broadcasted_iota(jnp.int32, (2 * HEAD_DIM, rows), 0)
    s_scr = (s0_scr, s1_scr)

    def cols(j):
        return pl.ds(j * rows, rows)

    def scores(j, m):
        qt = qt_ref[0, :, cols(j)]
        in_map = (feat >= m * HEAD_DIM) & (feat < (m + 1) * HEAD_DIM)
        qm = jnp.where(in_map, qt, jnp.zeros_like(qt))
        s = jnp.dot(k, qm, preferred_element_type=F32)
        s_scr[m][...] = s
        return jnp.max(s, axis=0, keepdims=True)

    def numerators(m, mx):
        return jnp.exp2(s_scr[m][...] - mx).astype(BF16)

    def weighted_values(p):
        o = jnp.dot(vt_ones, p, preferred_element_type=F32)
        return o[:2 * HEAD_DIM] / o[2 * HEAD_DIM:2 * HEAD_DIM + 1]

    def finish(j, o0, o1):
        o = o0 - lam * o1
        o = o * lax.rsqrt(jnp.mean(o * o, axis=0, keepdims=True) + SUBLN_EPS) * gain
        od_ref[0, cols(j), :] = o.T.astype(od_ref.dtype)

    qrows = DIL_Q_ROWS
    head0 = lax.broadcasted_iota(jnp.int32, (1, 1, LANES), 2) < HEAD_DIM

    def dilated_scores(j, pi):
        (window, d), bias_ref = DIL_PATTERNS[pi], (b0_ref, b1_ref, b2_ref)[pi]
        radius, L, wk, blocks = _dil_geometry(window, d, S)
        per = d * blocks // nblocks
        qsl, ksl = [], []
        for i in range(per):
            t = j * per + i
            r, q0 = t // blocks, (t % blocks) * qrows
            k0 = min(max(q0 - radius, 0), L - wk)
            qsl.append(pl.ds(q0 * d + r, qrows, stride=d))
            ksl.append(pl.ds(k0 * d + r, wk, stride=d))
        qt = jnp.stack([dq_ref[0, sl, :].astype(BF16) for sl in qsl])
        kt = jnp.stack([dk_ref[0, sl, :].astype(BF16) for sl in ksl])
        vt = jnp.stack([dv_ref[0, sl, :].astype(BF16) for sl in ksl])
        bias = bias_ref[pl.ds(j * per, per)]

        def per_head(x):
            zero = jnp.zeros_like(x)
            return jnp.concatenate([jnp.where(head0, x, zero), jnp.where(head0, zero, x)], axis=1)

        s = jnp.einsum("tqc,tkc->tqk", qt, per_head(kt), preferred_element_type=F32)
        ps, mxs, zs = [], [], []
        for h in range(2):
            sh = s[:, :, h * wk:(h + 1) * wk] + bias
            mx = jnp.max(sh, axis=-1, keepdims=True)
            p = jnp.exp2(sh - mx)
            ps.append(p.astype(BF16))
            mxs.append(mx)
            zs.append(jnp.sum(p, axis=-1, keepdims=True))
        mx = jnp.where(head0, mxs[0], mxs[1])
        z = jnp.where(head0, zs[0], zs[1])
        return pi, qsl, per_head(vt), jnp.concatenate(ps, axis=-1), mx, z

    def dilated_values(ctx):
        pi, qsl, v2, p, mx, z = ctx
        n = jnp.einsum("tqk,tkc->tqc", p, v2, preferred_element_type=F32)
        d = DIL_PATTERNS[pi][1]
        for i, dst in enumerate(qsl):
            if d % PAD_EVERY == 0:
                first = dst.start + dst.start // PAD_EVERY
                dst = pl.ds(first, qrows, stride=d + d // PAD_EVERY)
                wide_scr[0, dst, :] = n[i]
                wide_scr[1, dst, :] = mx[i]
                wide_scr[2, dst, :] = z[i]
            else:
                n_scr[pi, dst, :] = n[i]
                m_scr[pi, dst, :] = mx[i]
                z_scr[pi, dst, :] = z[i]

    groups = iter([(j, pi) for j in range(nblocks) for pi in range(len(DIL_PATTERNS))])

    def hosted(long_matmul):
        ctx = dilated_scores(*next(groups))
        out = long_matmul()
        dilated_values(ctx)
        return out

    mx0 = hosted(lambda: scores(0, 0))
    for j in range(nblocks):
        p0 = numerators(0, mx0)
        mx1 = hosted(lambda: scores(j, 1))
        o0 = weighted_values(p0)
        p1 = numerators(1, mx1)
        if j + 1 < nblocks:
            mx0 = hosted(lambda: scores(j + 1, 0))
        o1 = hosted(lambda: weighted_values(p1))
        finish(j, o0, o1)

    chunk = 256

    def stats(c, pi):
        if DIL_PATTERNS[pi][1] % PAD_EVERY:
            sl = pl.ds(c * chunk, chunk)
            return n_scr[pi, sl, :], m_scr[pi, sl, :], z_scr[pi, sl, :]
        groups16 = range(c * chunk // PAD_EVERY, (c + 1) * chunk // PAD_EVERY)
        return tuple(jnp.concatenate([wide_scr[k, pl.ds(g * (PAD_EVERY + 1), PAD_EVERY), :] for g in groups16],
                                     axis=0) for k in range(3))

    for c in range(S // chunk):
        parts = [stats(c, pi) for pi in range(len(DIL_PATTERNS))]
        top = functools.reduce(jnp.maximum, [m for _, m, _ in parts])
        num = jnp.zeros((chunk, LANES), F32)
        den = jnp.zeros((chunk, LANES), F32)
        for n, m, z in parts:
            a = jnp.exp2(m - top)
            num = num + a * n
            den = den + a * z
        og_ref[0, pl.ds(c * chunk, chunk), :] = (num / den).astype(og_ref.dtype)


def _attention(qt, kd, vt, dil_qkv, lq1, lk1, lq2, lk2, g_subln):
    B, S, _ = kd.shape
    nblk = D_DIL // LANES
    assert nblk == N_HEADS_DIFF
    small = lambda n: pl.BlockSpec((1, n), lambda b, h: (0, 0))
    biases = _dil_bias(S)
    const = lambda a: pl.BlockSpec(a.shape, lambda b, h: (0, 0, 0), pipeline_mode=pl.Buffered(1))
    s_scr = pltpu.VMEM((S, DIFF_Q_ROWS), F32)
    stat_scr = pltpu.VMEM((len(DIL_PATTERNS), S, LANES), F32)
    wide_scr = pltpu.VMEM((3, S + S // PAD_EVERY, LANES), F32)
    return pl.pallas_call(
        _attn_kernel,
        grid=(B, N_HEADS_DIFF),
        in_specs=[
            pl.BlockSpec((1, LANES, S), lambda b, h: (b, h, 0)),
            pl.BlockSpec((1, S, LANES), lambda b, h: (b, 0, h)),
            pl.BlockSpec((1, LANES, S), lambda b, h: (b, h, 0)),
            small(HEAD_DIM), small(HEAD_DIM), small(HEAD_DIM), small(HEAD_DIM),
            pl.BlockSpec((2 * HEAD_DIM, 1), lambda b, h: (0, 0)),
            pl.BlockSpec((1, S, LANES), lambda b, h: (b, 0, h)),
            pl.BlockSpec((1, S, LANES), lambda b, h: (b, 0, nblk + h)),
            pl.BlockSpec((1, S, LANES), lambda b, h: (b, 0, 2 * nblk + h)),
        ] + [const(a) for a in biases],
        out_specs=[
            pl.BlockSpec((1, S, LANES), lambda b, h: (b, 0, h)),
            pl.BlockSpec((1, S, LANES), lambda b, h: (b, 0, h)),
        ],
        out_shape=[
            jax.ShapeDtypeStruct((B, S, D_DIFF), BF16),
            jax.ShapeDtypeStruct((B, S, D_DIL), BF16),
        ],
        scratch_shapes=[s_scr, s_scr, stat_scr, stat_scr, stat_scr, wide_scr],
        compiler_params=pltpu.CompilerParams(
            dimension_semantics=("parallel", "parallel"),
            vmem_limit_bytes=VMEM_LIMIT_BYTES),
        name="attention",
    )(qt, kd, vt, lq1, lk1, lq2, lk2, g_subln.reshape(2 * HEAD_DIM, 1),
      dil_qkv, dil_qkv, dil_qkv, *biases)


def _post_kernel(x_ref, od_ref, og_ref, p_ref, wo_ref, gm_ref, wu_ref, wd_ref,
                 gp_ref, wg_ref, wp_ref, gf_ref, o_ref):
    half_rows = x_ref.shape[0] // POST_HALVES
    halves = [pl.ds(i * half_rows, half_rows) for i in range(POST_HALVES)]

    h = []
    for sl in halves:
        mixed = jnp.concatenate([od_ref[sl, :], og_ref[sl, :]], axis=-1)
        h.append(x_ref[sl, :] + jnp.dot(mixed, wo_ref[...], preferred_element_type=F32))

    for i in range(POST_HALVES):
        hn = _rms(h[i], gm_ref[...], NORM_EPS).astype(BF16)
        for c in range(D_FF // FF_CHUNK):
            u = jnp.dot(hn, wu_ref[:, c * FF_CHUNK:(c + 1) * FF_CHUNK], preferred_element_type=F32)
            u = jnp.square(jnp.maximum(u, 0.0)).astype(BF16)
            h[i] = h[i] + jnp.dot(u, wd_ref[c * FF_CHUNK:(c + 1) * FF_CHUNK, :],
                                  preferred_element_type=F32)

    for i, sl in enumerate(halves):
        hn = _rms(h[i], gp_ref[...], NORM_EPS).astype(BF16)
        gate = jax.nn.sigmoid(jnp.dot(hn, wg_ref[...], preferred_element_type=F32))
        emb = jnp.dot(p_ref[sl, :].astype(BF16), wp_ref[...], preferred_element_type=F32)
        o_ref[sl, :] = _rms(h[i] + gate * emb, gf_ref[...], NORM_EPS)


def _post(x2, od2, og2, p2, w_out, g_mlp, w_up, w_down, g_ple, w_gate, w_ple, g_final):
    N, D = x2.shape
    rows = POST_ROWS
    tok = lambda width: pl.BlockSpec((rows, width), lambda i: (i, 0))
    whole = lambda a: pl.BlockSpec(a.shape, lambda i: (0, 0), pipeline_mode=pl.Buffered(1))
    return pl.pallas_call(
        _post_kernel,
        grid=(N // rows,),
        in_specs=[
            tok(D), tok(D_DIFF), tok(D_DIL), tok(D_PLE),
            whole(w_out), whole(g_mlp), whole(w_up), whole(w_down),
            whole(g_ple), whole(w_gate), whole(w_ple), whole(g_final),
        ],
        out_specs=tok(D),
        out_shape=jax.ShapeDtypeStruct((N, D), F32),
        compiler_params=pltpu.CompilerParams(
            dimension_semantics=("parallel",),
            vmem_limit_bytes=VMEM_LIMIT_BYTES),
        name="out_mlp_ple",
    )(x2, od2, og2, p2, w_out, g_mlp, w_up, w_down, g_ple, w_gate, w_ple, g_final)


def _rope_tables(seq):
    inv = ROPE_THETA ** (-np.arange(0, HEAD_DIM, 2, dtype=np.float64) / HEAD_DIM)
    ang = np.arange(seq, dtype=np.float64)[:, None] * inv[None, :]
    cos, sin = np.cos(ang), np.sin(ang)
    reps = LANES // HEAD_DIM
    cos_t = np.tile(np.concatenate([cos, cos], axis=-1), (1, reps))
    sin_t = np.tile(np.concatenate([-sin, sin], axis=-1), (1, reps))
    return tuple(jnp.asarray(t, dtype=F32) for t in (cos_t, sin_t, cos.T, sin.T))


def kernel(x, p, w_in, w_out, g_mix, lambda_q1, lambda_k1, lambda_q2, lambda_k2, g_subln,
           g_mlp, w_up, w_down, g_ple, w_ple_gate, w_ple_proj, g_final):
    B, S, D = x.shape
    assert x.shape[1:] == (2048, D_MODEL) and w_in.shape == (1, D_MODEL, 2 * D_GROUP)

    later = (w_out[0], w_up[0], w_down[0], w_ple_gate[0], w_ple_proj[0])
    (qt, kd, vt, dil_qkv), (wo, wu, wd, wg, wp) = _project(x, g_mix, w_in[0], _rope_tables(S), later)
    od, og = _attention(qt, kd, vt, dil_qkv, lambda_q1, lambda_k1, lambda_q2, lambda_k2, g_subln)

    out = _post(
        x.reshape(B * S, D), od.reshape(B * S, D_DIFF), og.reshape(B * S, D_DIL),
        p[0].reshape(B * S, D_PLE),
        wo, g_mlp, wu, wd, g_ple, wg, wp, g_final.reshape(1, D))
    return out.reshape(B, S, D)
```

```python
import functools
import math

import jax
import jax.numpy as jnp
import numpy as np
from jax import lax
from jax.experimental import pallas as pl
from jax.experimental.pallas import tpu as pltpu

D_MODEL = 1024
HEAD_DIM = 64
D_DIFF = 512
D_DIL = 512
D_GROUP = 3 * D_DIFF
N_HEADS_DIFF = 4
DIL_PATTERNS = ((128, 1), (512, 4), (2048, 16))
D_FF = 4 * D_MODEL
D_PLE = 256
ROPE_THETA = 10000.0
NORM_EPS = 1e-6
SUBLN_EPS = 1e-5
NEG_INF = -1e30
LAM_INIT = 0.8 - 0.6 * math.exp(-0.3 * 0)
LOG2_E = math.log2(math.e)

LANES = 128
BF16_SUBLANES = 16
VMEM_LIMIT_BYTES = 56 * 1024 * 1024

PREP_ROWS = 256
PROJ_ROWS = 1024
PROJ_HALVES = 2
X_SLOTS = 3
DIFF_Q_ROWS = 512
DIL_Q_ROWS = 128
PAD_EVERY = 16
POST_ROWS = 512
POST_HALVES = 2
FF_CHUNK = 1024

BF16 = jnp.bfloat16
F32 = jnp.float32
NT_DIMS = (((1,), (1,)), ((), ()))


def _rms(x, g, eps):
    return x * lax.rsqrt(jnp.mean(x * x, axis=-1, keepdims=True) + eps) * g


def _proj_kernel(n_cast, x_ref, g_ref, wqt_ref, wvt_ref, w_ref, cos_ref, sin_ref, cost_ref, sint_ref,
                 *rest):
    x_buf, x_sem = rest[-2:]
    rest = rest[:-2]
    cast_in, (qt_ref, kd_ref, vt_ref, dil_ref), cast_out = rest[:n_cast], rest[n_cast:n_cast + 4], rest[n_cast + 4:]

    rows = x_buf.shape[1]
    tiles_per_seq = pl.num_programs(1)
    step = pl.program_id(0) * tiles_per_seq + pl.program_id(1)
    last_step = pl.num_programs(0) * tiles_per_seq - 1

    def x_copy(s):
        src = x_ref.at[s // tiles_per_seq, pl.ds((s % tiles_per_seq) * rows, rows), :]
        return pltpu.make_async_copy(src, x_buf.at[s % X_SLOTS], x_sem.at[s % X_SLOTS])

    @pl.when(step == 0)
    def _():
        for s in range(X_SLOTS - 1):
            x_copy(s).start()

    @pl.when(step + X_SLOTS - 1 <= last_step)
    def _():
        x_copy(step + X_SLOTS - 1).start()

    x_copy(step).wait()
    x_tile = x_buf.at[step % X_SLOTS]

    for src_ref, dst_ref in zip(cast_in, cast_out):
        dst_ref[...] = src_ref[...].astype(dst_ref.dtype)

    scale = HEAD_DIM ** -0.5 * LOG2_E
    half = HEAD_DIM // 2
    sub_rows = rows // PROJ_HALVES
    lane = lax.broadcasted_iota(jnp.int32, (sub_rows, LANES), 1)
    first_half = (lane % HEAD_DIM) < half
    slabs = ((kd_ref, 0, True, False), (dil_ref, 0, True, True),
             (dil_ref, D_DIL, True, False), (dil_ref, 2 * D_DIL, False, False))

    for i in range(PROJ_HALVES):
        sl = pl.ds(i * sub_rows, sub_rows)
        hn = _rms(x_tile[sl, :], g_ref[...], NORM_EPS).astype(BF16)
        cos = cos_ref[sl, :]
        sin = sin_ref[sl, :]

        def rope(t, cos=cos, sin=sin):
            swapped = jnp.where(first_half,
                                pltpu.roll(t, LANES - half, 1),
                                pltpu.roll(t, half, 1))
            return t * cos + swapped * sin

        qt = lax.dot_general(wqt_ref[...], hn, NT_DIMS, preferred_element_type=F32)
        cos_t = cost_ref[:, sl] * scale
        sin_t = sint_ref[:, sl] * scale
        for hb in range(D_DIFF // HEAD_DIM):
            t1 = qt[hb * HEAD_DIM:hb * HEAD_DIM + half]
            t2 = qt[hb * HEAD_DIM + half:(hb + 1) * HEAD_DIM]
            qt_ref[0, hb * HEAD_DIM:hb * HEAD_DIM + half, sl] = (t1 * cos_t - t2 * sin_t).astype(BF16)
            qt_ref[0, hb * HEAD_DIM + half:(hb + 1) * HEAD_DIM, sl] = (t1 * sin_t + t2 * cos_t).astype(BF16)

        vt_ref[0, :, sl] = lax.dot_general(wvt_ref[...], hn, NT_DIMS,
                                           preferred_element_type=F32).astype(BF16)

        for n, (out_ref, o0, use_rope, use_scale) in enumerate(slabs):
            acc = jnp.dot(hn, w_ref[:, n * D_DIFF:(n + 1) * D_DIFF], preferred_element_type=F32)
            for c in range(D_DIFF // LANES):
                t = acc[:, c * LANES:(c + 1) * LANES]
                if use_rope:
                    t = rope(t)
                if use_scale:
                    t = t * scale
                out_ref[0, sl, o0 + c * LANES:o0 + (c + 1) * LANES] = t.astype(out_ref.dtype)


def _prep_kernel(w_ref, wqt_ref, wvt_ref, wrest_ref):
    wqt_ref[...] = w_ref[:, :D_DIFF].T.astype(BF16)
    wvt_ref[...] = w_ref[:, 2 * D_DIFF:3 * D_DIFF].T.astype(BF16)
    wrest_ref[:, :D_DIFF] = w_ref[:, D_DIFF:2 * D_DIFF].astype(BF16)
    wrest_ref[:, D_DIFF:] = w_ref[:, D_GROUP:].astype(BF16)


def _prepare_w_in(w_in):
    D = w_in.shape[0]
    rows = PREP_ROWS
    return pl.pallas_call(
        _prep_kernel,
        grid=(D // rows,),
        in_specs=[pl.BlockSpec((rows, 2 * D_GROUP), lambda i: (i, 0))],
        out_specs=[
            pl.BlockSpec((D_DIFF, rows), lambda i: (0, i)),
            pl.BlockSpec((D_DIFF, rows), lambda i: (0, i)),
            pl.BlockSpec((rows, D_DIFF + D_GROUP), lambda i: (i, 0)),
        ],
        out_shape=[
            jax.ShapeDtypeStruct((D_DIFF, D), BF16),
            jax.ShapeDtypeStruct((D_DIFF, D), BF16),
            jax.ShapeDtypeStruct((D, D_DIFF + D_GROUP), BF16),
        ],
        compiler_params=pltpu.CompilerParams(dimension_semantics=("parallel",)),
        name="prep_w_in",
    )(w_in)


def _project(x, g_mix, w_in, tables, later_weights):
    B, S, D = x.shape
    rows = PROJ_ROWS
    steps = B * (S // rows)
    slab = lambda w: pl.BlockSpec((w.shape[0] // steps, w.shape[1]), lambda b, i: (b * (S // rows) + i, 0))
    cos_t, sin_t, cos_tt, sin_tt = tables
    wqt, wvt, w_rest = _prepare_w_in(w_in)
    const = lambda a: pl.BlockSpec(a.shape, lambda b, i: (0, 0), pipeline_mode=pl.Buffered(1))
    outs = pl.pallas_call(
        functools.partial(_proj_kernel, len(later_weights)),
        grid=(B, S // rows),
        in_specs=[
            pl.BlockSpec(memory_space=pl.ANY),
            pl.BlockSpec((1, D), lambda b, i: (0, 0)),
            const(wqt), const(wvt), const(w_rest),
            pl.BlockSpec((rows, LANES), lambda b, i: (i, 0)),
            pl.BlockSpec((rows, LANES), lambda b, i: (i, 0)),
            pl.BlockSpec((HEAD_DIM // 2, rows), lambda b, i: (0, i)),
            pl.BlockSpec((HEAD_DIM // 2, rows), lambda b, i: (0, i)),
        ] + [slab(w) for w in later_weights],
        out_specs=[
            pl.BlockSpec((1, D_DIFF, rows), lambda b, i: (b, 0, i)),
            pl.BlockSpec((1, rows, D_DIFF), lambda b, i: (b, i, 0)),
            pl.BlockSpec((1, D_DIFF, rows), lambda b, i: (b, 0, i)),
            pl.BlockSpec((1, rows, D_GROUP), lambda b, i: (b, i, 0)),
        ] + [slab(w) for w in later_weights],
        out_shape=[
            jax.ShapeDtypeStruct((B, D_DIFF, S), BF16),
            jax.ShapeDtypeStruct((B, S, D_DIFF), BF16),
            jax.ShapeDtypeStruct((B, D_DIFF, S), BF16),
            jax.ShapeDtypeStruct((B, S, D_GROUP), F32),
        ] + [jax.ShapeDtypeStruct(w.shape, BF16) for w in later_weights],
        scratch_shapes=[pltpu.VMEM((X_SLOTS, rows, D), F32), pltpu.SemaphoreType.DMA((X_SLOTS,))],
        compiler_params=pltpu.CompilerParams(
            dimension_semantics=("arbitrary", "arbitrary"),
            vmem_limit_bytes=VMEM_LIMIT_BYTES),
        name="proj_rope",
    )(x, g_mix, wqt, wvt, w_rest, cos_t, sin_t, cos_tt, sin_tt, *later_weights)
    return outs[:4], outs[4:]


def _dil_geometry(window, d, seq):
    radius, L = window // (2 * d), seq // d
    wk = min(DIL_Q_ROWS + 2 * radius, L)
    return radius, L, wk, L // DIL_Q_ROWS


def _dil_bias(seq):
    out = []
    for window, d in DIL_PATTERNS:
        radius, L, wk, blocks = _dil_geometry(window, d, seq)
        row = np.arange(DIL_Q_ROWS)[:, None]
        col = np.arange(wk)[None, :]
        bias = []
        for t in range(d * blocks):
            q0 = (t % blocks) * DIL_Q_ROWS
            k0 = min(max(q0 - radius, 0), L - wk)
            bias.append(np.where(np.abs((k0 + col) - (q0 + row)) <= radius, 0.0, NEG_INF))
        out.append(jnp.asarray(np.stack(bias), dtype=F32))
    return out


def _attn_kernel(qt_ref, k_ref, vt_ref, lq1_ref, lk1_ref, lq2_ref, lk2_ref, g_ref,
                 dq_ref, dk_ref, dv_ref, b0_ref, b1_ref, b2_ref,
                 od_ref, og_ref, s0_scr, s1_scr, n_scr, m_scr, z_scr, wide_scr):
    S = k_ref.shape[1]
    rows = DIFF_Q_ROWS
    nblocks = S // rows
    k = k_ref[0]
    vt_ones = jnp.concatenate([vt_ref[0], jnp.ones((BF16_SUBLANES, S), BF16)], axis=0)
    lam = (jnp.exp(jnp.sum(lq1_ref[...] * lk1_ref[...], axis=-1, keepdims=True))
           - jnp.exp(jnp.sum(lq2_ref[...] * lk2_ref[...], axis=-1, keepdims=True))
           + LAM_INIT)
    gain = g_ref[...] * (1.0 - LAM_INIT)
    feat = lax.broadcasted_iota(jnp.int32, (2 * HEAD_DIM, rows), 0)
    s_scr = (s0_scr, s1_scr)

    def cols(j):
        return pl.ds(j * rows, rows)

    def scores(j, m):
        qt = qt_ref[0, :, cols(j)]
        in_map = (feat >= m * HEAD_DIM) & (feat < (m + 1) * HEAD_DIM)
        qm = jnp.where(in_map, qt, jnp.zeros_like(qt))
        s = jnp.dot(k, qm, preferred_element_type=F32)
        s_scr[m][...] = s
        return jnp.max(s, axis=0, keepdims=True)

    def numerators(m, mx):
        return jnp.exp2(s_scr[m][...] - mx).astype(BF16)

    def weighted_values(p):
        o = jnp.dot(vt_ones, p, preferred_element_type=F32)
        return o[:2 * HEAD_DIM] / o[2 * HEAD_DIM:2 * HEAD_DIM + 1]

    def finish(j, o0, o1):
        o = o0 - lam * o1
        o = o * lax.rsqrt(jnp.mean(o * o, axis=0, keepdims=True) + SUBLN_EPS) * gain
        od_ref[0, cols(j), :] = o.T.astype(od_ref.dtype)

    qrows = DIL_Q_ROWS
    head0 = lax.broadcasted_iota(jnp.int32, (1, 1, LANES), 2) < HEAD_DIM

    def dilated_scores(j, pi):
        (window, d), bias_ref = DIL_PATTERNS[pi], (b0_ref, b1_ref, b2_ref)[pi]
        radius, L, wk, blocks = _dil_geometry(window, d, S)
        per = d * blocks // nblocks
        qsl, ksl = [], []
        for i in range(per):
            t = j * per + i
            r, q0 = t // blocks, (t % blocks) * qrows
            k0 = min(max(q0 - radius, 0), L - wk)
            qsl.append(pl.ds(q0 * d + r, qrows, stride=d))
            ksl.append(pl.ds(k0 * d + r, wk, stride=d))
        qt = jnp.stack([dq_ref[0, sl, :].astype(BF16) for sl in qsl])
        kt = jnp.stack([dk_ref[0, sl, :].astype(BF16) for sl in ksl])
        vt = jnp.stack([dv_ref[0, sl, :].astype(BF16) for sl in ksl])
        bias = bias_ref[pl.ds(j * per, per)]

        def per_head(x):
            zero = jnp.zeros_like(x)
            return jnp.concatenate([jnp.where(head0, x, zero), jnp.where(head0, zero, x)], axis=1)

        s = jnp.einsum("tqc,tkc->tqk", qt, per_head(kt), preferred_element_type=F32)
        ps, mxs, zs = [], [], []
        for h in range(2):
            sh = s[:, :, h * wk:(h + 1) * wk] + bias
            mx = jnp.max(sh, axis=-1, keepdims=True)
            p = jnp.exp2(sh - mx)
            ps.append(p.astype(BF16))
            mxs.append(mx)
            zs.append(jnp.sum(p, axis=-1, keepdims=True))
        mx = jnp.where(head0, mxs[0], mxs[1])
        z = jnp.where(head0, zs[0], zs[1])
        return pi, qsl, per_head(vt), jnp.concatenate(ps, axis=-1), mx, z

    def dilated_values(ctx):
        pi, qsl, v2, p, mx, z = ctx
        n = jnp.einsum("tqk,tkc->tqc", p, v2, preferred_element_type=F32)
        d = DIL_PATTERNS[pi][1]
        for i, dst in enumerate(qsl):
            if d % PAD_EVERY == 0:
                first = dst.start + dst.start // PAD_EVERY
                dst = pl.ds(first, qrows, stride=d + d // PAD_EVERY)
                wide_scr[0, dst, :] = n[i]
                wide_scr[1, dst, :] = mx[i]
                wide_scr[2, dst, :] = z[i]
            else:
                n_scr[pi, dst, :] = n[i]
                m_scr[pi, dst, :] = mx[i]
                z_scr[pi, dst, :] = z[i]

    groups = iter([(j, pi) for j in range(nblocks) for pi in range(len(DIL_PATTERNS))])

    def hosted(long_matmul):
        ctx = dilated_scores(*next(groups))
        out = long_matmul()
        dilated_values(ctx)
        return out

    mx0 = hosted(lambda: scores(0, 0))
    for j in range(nblocks):
        p0 = numerators(0, mx0)
        mx1 = hosted(lambda: scores(j, 1))
        o0 = weighted_values(p0)
        p1 = numerators(1, mx1)
        if j + 1 < nblocks:
            mx0 = hosted(lambda: scores(j + 1, 0))
        o1 = hosted(lambda: weighted_values(p1))
        finish(j, o0, o1)

    chunk = 256

    def stats(c, pi):
        if DIL_PATTERNS[pi][1] % PAD_EVERY:
            sl = pl.ds(c * chunk, chunk)
            return n_scr[pi, sl, :], m_scr[pi, sl, :], z_scr[pi, sl, :]
        groups16 = range(c * chunk // PAD_EVERY, (c + 1) * chunk // PAD_EVERY)
        return tuple(jnp.concatenate([wide_scr[k, pl.ds(g * (PAD_EVERY + 1), PAD_EVERY), :] for g in groups16],
                                     axis=0) for k in range(3))

    for c in range(S // chunk):
        parts = [stats(c, pi) for pi in range(len(DIL_PATTERNS))]
        top = functools.reduce(jnp.maximum, [m for _, m, _ in parts])
        num = jnp.zeros((chunk, LANES), F32)
        den = jnp.zeros((chunk, LANES), F32)
        for n, m, z in parts:
            a = jnp.exp2(m - top)
            num = num + a * n
            den = den + a * z
        og_ref[0, pl.ds(c * chunk, chunk), :] = (num / den).astype(og_ref.dtype)


def _attention(qt, kd, vt, dil_qkv, lq1, lk1, lq2, lk2, g_subln):
    B, S, _ = kd.shape
    nblk = D_DIL // LANES
    assert nblk == N_HEADS_DIFF
    small = lambda n: pl.BlockSpec((1, n), lambda b, h: (0, 0))
    biases = _dil_bias(S)
    const = lambda a: pl.BlockSpec(a.shape, lambda b, h: (0, 0, 0), pipeline_mode=pl.Buffered(1))
    s_scr = pltpu.VMEM((S, DIFF_Q_ROWS), F32)
    stat_scr = pltpu.VMEM((len(DIL_PATTERNS), S, LANES), F32)
    wide_scr = pltpu.VMEM((3, S + S // PAD_EVERY, LANES), F32)
    return pl.pallas_call(
        _attn_kernel,
        grid=(B, N_HEADS_DIFF),
        in_specs=[
            pl.BlockSpec((1, LANES, S), lambda b, h: (b, h, 0)),
            pl.BlockSpec((1, S, LANES), lambda b, h: (b, 0, h)),
            pl.BlockSpec((1, LANES, S), lambda b, h: (b, h, 0)),
            small(HEAD_DIM), small(HEAD_DIM), small(HEAD_DIM), small(HEAD_DIM),
            pl.BlockSpec((2 * HEAD_DIM, 1), lambda b, h: (0, 0)),
            pl.BlockSpec((1, S, LANES), lambda b, h: (b, 0, h)),
            pl.BlockSpec((1, S, LANES), lambda b, h: (b, 0, nblk + h)),
            pl.BlockSpec((1, S, LANES), lambda b, h: (b, 0, 2 * nblk + h)),
        ] + [const(a) for a in biases],
        out_specs=[
            pl.BlockSpec((1, S, LANES), lambda b, h: (b, 0, h)),
            pl.BlockSpec((1, S, LANES), lambda b, h: (b, 0, h)),
        ],
        out_shape=[
            jax.ShapeDtypeStruct((B, S, D_DIFF), BF16),
            jax.ShapeDtypeStruct((B, S, D_DIL), BF16),
        ],
        scratch_shapes=[s_scr, s_scr, stat_scr, stat_scr, stat_scr, wide_scr],
        compiler_params=pltpu.CompilerParams(
            dimension_semantics=("parallel", "parallel"),
            vmem_limit_bytes=VMEM_LIMIT_BYTES),
        name="attention",
    )(qt, kd, vt, lq1, lk1, lq2, lk2, g_subln.reshape(2 * HEAD_DIM, 1),
      dil_qkv, dil_qkv, dil_qkv, *biases)


def _post_kernel(x_ref, od_ref, og_ref, p_ref, wo_ref, gm_ref, wu_ref, wd_ref,
                 gp_ref, wg_ref, wp_ref, gf_ref, o_ref):
    half_rows = x_ref.shape[0] // POST_HALVES
    halves = [pl.ds(i * half_rows, half_rows) for i in range(POST_HALVES)]

    h = []
    for sl in halves:
        mixed = jnp.concatenate([od_ref[sl, :], og_ref[sl, :]], axis=-1)
        h.append(x_ref[sl, :] + jnp.dot(mixed, wo_ref[...], preferred_element_type=F32))

    for i in range(POST_HALVES):
        hn = _rms(h[i], gm_ref[...], NORM_EPS).astype(BF16)
        for c in range(D_FF // FF_CHUNK):
            u = jnp.dot(hn, wu_ref[:, c * FF_CHUNK:(c + 1) * FF_CHUNK], preferred_element_type=F32)
            u = jnp.square(jnp.maximum(u, 0.0)).astype(BF16)
            h[i] = h[i] + jnp.dot(u, wd_ref[c * FF_CHUNK:(c + 1) * FF_CHUNK, :],
                                  preferred_element_type=F32)

    for i, sl in enumerate(halves):
        hn = _rms(h[i], gp_ref[...], NORM_EPS).astype(BF16)
        gate = jax.nn.sigmoid(jnp.dot(hn, wg_ref[...], preferred_element_type=F32))
        emb = jnp.dot(p_ref[sl, :].astype(BF16), wp_ref[...], preferred_element_type=F32)
        o_ref[sl, :] = _rms(h[i] + gate * emb, gf_ref[...], NORM_EPS)


def _post(x2, od2, og2, p2, w_out, g_mlp, w_up, w_down, g_ple, w_gate, w_ple, g_final):
    N, D = x2.shape
    rows = POST_ROWS
    tok = lambda width: pl.BlockSpec((rows, width), lambda i: (i, 0))
    whole = lambda a: pl.BlockSpec(a.shape, lambda i: (0, 0), pipeline_mode=pl.Buffered(1))
    return pl.pallas_call(
        _post_kernel,
        grid=(N // rows,),
        in_specs=[
            tok(D), tok(D_DIFF), tok(D_DIL), tok(D_PLE),
            whole(w_out), whole(g_mlp), whole(w_up), whole(w_down),
            whole(g_ple), whole(w_gate), whole(w_ple), whole(g_final),
        ],
        out_specs=tok(D),
        out_shape=jax.ShapeDtypeStruct((N, D), F32),
        compiler_params=pltpu.CompilerParams(
            dimension_semantics=("parallel",),
            vmem_limit_bytes=VMEM_LIMIT_BYTES),
        name="out_mlp_ple",
    )(x2, od2, og2, p2, w_out, g_mlp, w_up, w_down, g_ple, w_gate, w_ple, g_final)


def _rope_tables(seq):
    inv = ROPE_THETA ** (-np.arange(0, HEAD_DIM, 2, dtype=np.float64) / HEAD_DIM)
    ang = np.arange(seq, dtype=np.float64)[:, None] * inv[None, :]
    cos, sin = np.cos(ang), np.sin(ang)
    reps = LANES // HEAD_DIM
    cos_t = np.tile(np.concatenate([cos, cos], axis=-1), (1, reps))
    sin_t = np.tile(np.concatenate([-sin, sin], axis=-1), (1, reps))
    return tuple(jnp.asarray(t, dtype=F32) for t in (cos_t, sin_t, cos.T, sin.T))


def kernel(x, p, w_in, w_out, g_mix, lambda_q1, lambda_k1, lambda_q2, lambda_k2, g_subln,
           g_mlp, w_up, w_down, g_ple, w_ple_gate, w_ple_proj, g_final):
    B, S, D = x.shape
    assert x.shape[1:] == (2048, D_MODEL) and w_in.shape == (1, D_MODEL, 2 * D_GROUP)

    later = (w_out[0], w_up[0], w_down[0], w_ple_gate[0], w_ple_proj[0])
    (qt, kd, vt, dil_qkv), (wo, wu, wd, wg, wp) = _project(x, g_mix, w_in[0], _rope_tables(S), later)
    od, og = _attention(qt, kd, vt, dil_qkv, lambda_q1, lambda_k1, lambda_q2, lambda_k2, g_subln)

    out = _post(
        x.reshape(B * S, D), od.reshape(B * S, D_DIFF), og.reshape(B * S, D_DIL),
        p[0].reshape(B * S, D_PLE),
        wo, g_mlp, wu, wd, g_ple, wg, wp, g_final.reshape(1, D))
    return out.reshape(B, S, D)
```

```python
import functools
import math

import jax
import jax.numpy as jnp
import numpy as np
from jax import lax
from jax.experimental import pallas as pl
from jax.experimental.pallas import tpu as pltpu

D_MODEL = 1024
HEAD_DIM = 64
D_DIFF = 512
D_DIL = 512
D_GROUP = 3 * D_DIFF
N_HEADS_DIFF = 4
DIL_PATTERNS = ((128, 1), (512, 4), (2048, 16))
D_FF = 4 * D_MODEL
D_PLE = 256
ROPE_THETA = 10000.0
NORM_EPS = 1e-6
SUBLN_EPS = 1e-5
NEG_INF = -1e30
LAM_INIT = 0.8 - 0.6 * math.exp(-0.3 * 0)
LOG2_E = math.log2(math.e)

LANES = 128
BF16_SUBLANES = 16
VMEM_LIMIT_BYTES = 56 * 1024 * 1024

PREP_ROWS = 256
PROJ_ROWS = 1024
PROJ_HALVES = 2
DIFF_Q_ROWS = 512
DIL_Q_ROWS = 128
PAD_EVERY = 16
POST_ROWS = 512
POST_HALVES = 2
FF_CHUNK = 1024

BF16 = jnp.bfloat16
F32 = jnp.float32
NT_DIMS = (((1,), (1,)), ((), ()))


def _rms(x, g, eps):
    return x * lax.rsqrt(jnp.mean(x * x, axis=-1, keepdims=True) + eps) * g


def _proj_kernel(n_cast, x_ref, g_ref, wqt_ref, wvt_ref, w_ref, cos_ref, sin_ref, cost_ref, sint_ref,
                 *rest):
    cast_in, (qt_ref, kd_ref, vt_ref, dil_ref), cast_out = rest[:n_cast], rest[n_cast:n_cast + 4], rest[n_cast + 4:]
    for src_ref, dst_ref in zip(cast_in, cast_out):
        dst_ref[...] = src_ref[...].astype(dst_ref.dtype)

    scale = HEAD_DIM ** -0.5 * LOG2_E
    half = HEAD_DIM // 2
    sub_rows = x_ref.shape[1] // PROJ_HALVES
    lane = lax.broadcasted_iota(jnp.int32, (sub_rows, LANES), 1)
    first_half = (lane % HEAD_DIM) < half
    slabs = ((kd_ref, 0, True, False), (dil_ref, 0, True, True),
             (dil_ref, D_DIL, True, False), (dil_ref, 2 * D_DIL, False, False))

    for i in range(PROJ_HALVES):
        sl = pl.ds(i * sub_rows, sub_rows)
        hn = _rms(x_ref[0, sl, :], g_ref[...], NORM_EPS).astype(BF16)
        cos = cos_ref[sl, :]
        sin = sin_ref[sl, :]

        def rope(t, cos=cos, sin=sin):
            swapped = jnp.where(first_half,
                                pltpu.roll(t, LANES - half, 1),
                                pltpu.roll(t, half, 1))
            return t * cos + swapped * sin

        qt = lax.dot_general(wqt_ref[...], hn, NT_DIMS, preferred_element_type=F32)
        cos_t = cost_ref[:, sl] * scale
        sin_t = sint_ref[:, sl] * scale
        for hb in range(D_DIFF // HEAD_DIM):
            t1 = qt[hb * HEAD_DIM:hb * HEAD_DIM + half]
            t2 = qt[hb * HEAD_DIM + half:(hb + 1) * HEAD_DIM]
            qt_ref[0, hb * HEAD_DIM:hb * HEAD_DIM + half, sl] = (t1 * cos_t - t2 * sin_t).astype(BF16)
            qt_ref[0, hb * HEAD_DIM + half:(hb + 1) * HEAD_DIM, sl] = (t1 * sin_t + t2 * cos_t).astype(BF16)

        vt_ref[0, :, sl] = lax.dot_general(wvt_ref[...], hn, NT_DIMS,
                                           preferred_element_type=F32).astype(BF16)

        for n, (out_ref, o0, use_rope, use_scale) in enumerate(slabs):
            acc = jnp.dot(hn, w_ref[:, n * D_DIFF:(n + 1) * D_DIFF], preferred_element_type=F32)
            for c in range(D_DIFF // LANES):
                t = acc[:, c * LANES:(c + 1) * LANES]
                if use_rope:
                    t = rope(t)
                if use_scale:
                    t = t * scale
                out_ref[0, sl, o0 + c * LANES:o0 + (c + 1) * LANES] = t.astype(out_ref.dtype)


def _prep_kernel(w_ref, wqt_ref, wvt_ref, wrest_ref):
    wqt_ref[...] = w_ref[:, :D_DIFF].T.astype(BF16)
    wvt_ref[...] = w_ref[:, 2 * D_DIFF:3 * D_DIFF].T.astype(BF16)
    wrest_ref[:, :D_DIFF] = w_ref[:, D_DIFF:2 * D_DIFF].astype(BF16)
    wrest_ref[:, D_DIFF:] = w_ref[:, D_GROUP:].astype(BF16)


def _prepare_w_in(w_in):
    D = w_in.shape[0]
    rows = PREP_ROWS
    return pl.pallas_call(
        _prep_kernel,
        grid=(D // rows,),
        in_specs=[pl.BlockSpec((rows, 2 * D_GROUP), lambda i: (i, 0))],
        out_specs=[
            pl.BlockSpec((D_DIFF, rows), lambda i: (0, i)),
            pl.BlockSpec((D_DIFF, rows), lambda i: (0, i)),
            pl.BlockSpec((rows, D_DIFF + D_GROUP), lambda i: (i, 0)),
        ],
        out_shape=[
            jax.ShapeDtypeStruct((D_DIFF, D), BF16),
            jax.ShapeDtypeStruct((D_DIFF, D), BF16),
            jax.ShapeDtypeStruct((D, D_DIFF + D_GROUP), BF16),
        ],
        compiler_params=pltpu.CompilerParams(dimension_semantics=("parallel",)),
        name="prep_w_in",
    )(w_in)


def _project(x, g_mix, w_in, tables, later_weights):
    B, S, D = x.shape
    rows = PROJ_ROWS
    steps = B * (S // rows)
    slab = lambda w: pl.BlockSpec((w.shape[0] // steps, w.shape[1]), lambda b, i: (b * (S // rows) + i, 0))
    cos_t, sin_t, cos_tt, sin_tt = tables
    wqt, wvt, w_rest = _prepare_w_in(w_in)
    const = lambda a: pl.BlockSpec(a.shape, lambda b, i: (0, 0), pipeline_mode=pl.Buffered(1))
    outs = pl.pallas_call(
        functools.partial(_proj_kernel, len(later_weights)),
        grid=(B, S // rows),
        in_specs=[
            pl.BlockSpec((1, rows, D), lambda b, i: (b, i, 0)),
            pl.BlockSpec((1, D), lambda b, i: (0, 0)),
            const(wqt), const(wvt), const(w_rest),
            pl.BlockSpec((rows, LANES), lambda b, i: (i, 0)),
            pl.BlockSpec((rows, LANES), lambda b, i: (i, 0)),
            pl.BlockSpec((HEAD_DIM // 2, rows), lambda b, i: (0, i)),
            pl.BlockSpec((HEAD_DIM // 2, rows), lambda b, i: (0, i)),
        ] + [slab(w) for w in later_weights],
        out_specs=[
            pl.BlockSpec((1, D_DIFF, rows), lambda b, i: (b, 0, i)),
            pl.BlockSpec((1, rows, D_DIFF), lambda b, i: (b, i, 0)),
            pl.BlockSpec((1, D_DIFF, rows), lambda b, i: (b, 0, i)),
            pl.BlockSpec((1, rows, D_GROUP), lambda b, i: (b, i, 0)),
        ] + [slab(w) for w in later_weights],
        out_shape=[
            jax.ShapeDtypeStruct((B, D_DIFF, S), BF16),
            jax.ShapeDtypeStruct((B, S, D_DIFF), BF16),
            jax.ShapeDtypeStruct((B, D_DIFF, S), BF16),
            jax.ShapeDtypeStruct((B, S, D_GROUP), F32),
        ] + [jax.ShapeDtypeStruct(w.shape, BF16) for w in later_weights],
        compiler_params=pltpu.CompilerParams(
            dimension_semantics=("parallel", "parallel"),
            vmem_limit_bytes=VMEM_LIMIT_BYTES),
        name="proj_rope",
    )(x, g_mix, wqt, wvt, w_rest, cos_t, sin_t, cos_tt, sin_tt, *later_weights)
    return outs[:4], outs[4:]


def _dil_geometry(window, d, seq):
    radius, L = window // (2 * d), seq // d
    wk = min(DIL_Q_ROWS + 2 * radius, L)
    return radius, L, wk, L // DIL_Q_ROWS


def _dil_bias(seq):
    out = []
    for window, d in DIL_PATTERNS:
        radius, L, wk, blocks = _dil_geometry(window, d, seq)
        row = np.arange(DIL_Q_ROWS)[:, None]
        col = np.arange(wk)[None, :]
        bias = []
        for t in range(d * blocks):
            q0 = (t % blocks) * DIL_Q_ROWS
            k0 = min(max(q0 - radius, 0), L - wk)
            bias.append(np.where(np.abs((k0 + col) - (q0 + row)) <= radius, 0.0, NEG_INF))
        out.append(jnp.asarray(np.stack(bias), dtype=F32))
    return out


def _attn_kernel(qt_ref, k_ref, vt_ref, lq1_ref, lk1_ref, lq2_ref, lk2_ref, g_ref,
                 dq_ref, dk_ref, dv_ref, b0_ref, b1_ref, b2_ref,
                 od_ref, og_ref, s0_scr, s1_scr, n_scr, m_scr, z_scr, wide_scr):
    S = k_ref.shape[1]
    rows = DIFF_Q_ROWS
    nblocks = S // rows
    vt_ones = jnp.concatenate([vt_ref[0], jnp.ones((BF16_SUBLANES, S), BF16)], axis=0)
    lam = (jnp.exp(jnp.sum(lq1_ref[...] * lk1_ref[...], axis=-1, keepdims=True))
           - jnp.exp(jnp.sum(lq2_ref[...] * lk2_ref[...], axis=-1, keepdims=True))
           + LAM_INIT)
    gain = g_ref[...] * (1.0 - LAM_INIT)
    feat = lax.broadcasted_iota(jnp.int32, (2 * HEAD_DIM, rows), 0)
    s_scr = (s0_scr, s1_scr)

    def cols(j):
        return pl.ds(j * rows, rows)

    def scores(j, m):
        qt = qt_ref[0, :, cols(j)]
        in_map = (feat >= m * HEAD_DIM) & (feat < (m + 1) * HEAD_DIM)
        qm = jnp.where(in_map, qt, jnp.zeros_like(qt))
        s = jnp.dot(k_ref[0], qm, preferred_element_type=F32)
        s_scr[m][...] = s
        return jnp.max(s, axis=0, keepdims=True)

    def numerators(m, mx):
        return jnp.exp2(s_scr[m][...] - mx).astype(BF16)

    def weighted_values(p):
        o = jnp.dot(vt_ones, p, preferred_element_type=F32)
        return o[:2 * HEAD_DIM] / o[2 * HEAD_DIM:2 * HEAD_DIM + 1]

    def finish(j, o0, o1):
        o = o0 - lam * o1
        o = o * lax.rsqrt(jnp.mean(o * o, axis=0, keepdims=True) + SUBLN_EPS) * gain
        od_ref[0, cols(j), :] = o.T.astype(od_ref.dtype)

    qrows = DIL_Q_ROWS
    head0 = lax.broadcasted_iota(jnp.int32, (1, 1, LANES), 2) < HEAD_DIM

    def dilated_scores(j, pi):
        (window, d), bias_ref = DIL_PATTERNS[pi], (b0_ref, b1_ref, b2_ref)[pi]
        radius, L, wk, blocks = _dil_geometry(window, d, S)
        per = d * blocks // nblocks
        qsl, ksl = [], []
        for i in range(per):
            t = j * per + i
            r, q0 = t // blocks, (t % blocks) * qrows
            k0 = min(max(q0 - radius, 0), L - wk)
            qsl.append(pl.ds(q0 * d + r, qrows, stride=d))
            ksl.append(pl.ds(k0 * d + r, wk, stride=d))
        qt = jnp.stack([dq_ref[0, sl, :].astype(BF16) for sl in qsl])
        kt = jnp.stack([dk_ref[0, sl, :].astype(BF16) for sl in ksl])
        vt = jnp.stack([dv_ref[0, sl, :].astype(BF16) for sl in ksl])
        bias = bias_ref[pl.ds(j * per, per)]

        def per_head(x):
            zero = jnp.zeros_like(x)
            return jnp.concatenate([jnp.where(head0, x, zero), jnp.where(head0, zero, x)], axis=1)

        s = jnp.einsum("tqc,tkc->tqk", qt, per_head(kt), preferred_element_type=F32)
        ps, mxs, zs = [], [], []
        for h in range(2):
            sh = s[:, :, h * wk:(h + 1) * wk] + bias
            mx = jnp.max(sh, axis=-1, keepdims=True)
            p = jnp.exp2(sh - mx)
            ps.append(p.astype(BF16))
            mxs.append(mx)
            zs.append(jnp.sum(p, axis=-1, keepdims=True))
        mx = jnp.where(head0, mxs[0], mxs[1])
        z = jnp.where(head0, zs[0], zs[1])
        return pi, qsl, per_head(vt), jnp.concatenate(ps, axis=-1), mx, z

    def dilated_values(ctx):
        pi, qsl, v2, p, mx, z = ctx
        n = jnp.einsum("tqk,tkc->tqc", p, v2, preferred_element_type=F32)
        d = DIL_PATTERNS[pi][1]
        for i, dst in enumerate(qsl):
            if d % PAD_EVERY == 0:
                first = dst.start + dst.start // PAD_EVERY
                dst = pl.ds(first, qrows, stride=d + d // PAD_EVERY)
                wide_scr[0, dst, :] = n[i]
                wide_scr[1, dst, :] = mx[i]
                wide_scr[2, dst, :] = z[i]
            else:
                n_scr[pi, dst, :] = n[i]
                m_scr[pi, dst, :] = mx[i]
                z_scr[pi, dst, :] = z[i]

    groups = iter([(j, pi) for j in range(nblocks) for pi in range(len(DIL_PATTERNS))])

    def hosted(long_matmul):
        ctx = dilated_scores(*next(groups))
        out = long_matmul()
        dilated_values(ctx)
        return out

    mx0 = hosted(lambda: scores(0, 0))
    for j in range(nblocks):
        p0 = numerators(0, mx0)
        mx1 = hosted(lambda: scores(j, 1))
        o0 = weighted_values(p0)
        p1 = numerators(1, mx1)
        if j + 1 < nblocks:
            mx0 = hosted(lambda: scores(j + 1, 0))
        o1 = hosted(lambda: weighted_values(p1))
        finish(j, o0, o1)

    chunk = 256

    def stats(c, pi):
        if DIL_PATTERNS[pi][1] % PAD_EVERY:
            sl = pl.ds(c * chunk, chunk)
            return n_scr[pi, sl, :], m_scr[pi, sl, :], z_scr[pi, sl, :]
        groups16 = range(c * chunk // PAD_EVERY, (c + 1) * chunk // PAD_EVERY)
        return tuple(jnp.concatenate([wide_scr[k, pl.ds(g * (PAD_EVERY + 1), PAD_EVERY), :] for g in groups16],
                                     axis=0) for k in range(3))

    for c in range(S // chunk):
        parts = [stats(c, pi) for pi in range(len(DIL_PATTERNS))]
        top = functools.reduce(jnp.maximum, [m for _, m, _ in parts])
        num = jnp.zeros((chunk, LANES), F32)
        den = jnp.zeros((chunk, LANES), F32)
        for n, m, z in parts:
            a = jnp.exp2(m - top)
            num = num + a * n
            den = den + a * z
        og_ref[0, pl.ds(c * chunk, chunk), :] = (num / den).astype(og_ref.dtype)


def _attention(qt, kd, vt, dil_qkv, lq1, lk1, lq2, lk2, g_subln):
    B, S, _ = kd.shape
    nblk = D_DIL // LANES
    assert nblk == N_HEADS_DIFF
    small = lambda n: pl.BlockSpec((1, n), lambda b, h: (0, 0))
    biases = _dil_bias(S)
    const = lambda a: pl.BlockSpec(a.shape, lambda b, h: (0, 0, 0), pipeline_mode=pl.Buffered(1))
    s_scr = pltpu.VMEM((S, DIFF_Q_ROWS), F32)
    stat_scr = pltpu.VMEM((len(DIL_PATTERNS), S, LANES), F32)
    wide_scr = pltpu.VMEM((3, S + S // PAD_EVERY, LANES), F32)
    return pl.pallas_call(
        _attn_kernel,
        grid=(B, N_HEADS_DIFF),
        in_specs=[
            pl.BlockSpec((1, LANES, S), lambda b, h: (b, h, 0)),
            pl.BlockSpec((1, S, LANES), lambda b, h: (b, 0, h)),
            pl.BlockSpec((1, LANES, S), lambda b, h: (b, h, 0)),
            small(HEAD_DIM), small(HEAD_DIM), small(HEAD_DIM), small(HEAD_DIM),
            pl.BlockSpec((2 * HEAD_DIM, 1), lambda b, h: (0, 0)),
            pl.BlockSpec((1, S, LANES), lambda b, h: (b, 0, h)),
            pl.BlockSpec((1, S, LANES), lambda b, h: (b, 0, nblk + h)),
            pl.BlockSpec((1, S, LANES), lambda b, h: (b, 0, 2 * nblk + h)),
        ] + [const(a) for a in biases],
        out_specs=[
            pl.BlockSpec((1, S, LANES), lambda b, h: (b, 0, h)),
            pl.BlockSpec((1, S, LANES), lambda b, h: (b, 0, h)),
        ],
        out_shape=[
            jax.ShapeDtypeStruct((B, S, D_DIFF), BF16),
            jax.ShapeDtypeStruct((B, S, D_DIL), BF16),
        ],
        scratch_shapes=[s_scr, s_scr, stat_scr, stat_scr, stat_scr, wide_scr],
        compiler_params=pltpu.CompilerParams(
            dimension_semantics=("parallel", "parallel"),
            vmem_limit_bytes=VMEM_LIMIT_BYTES),
        name="attention",
    )(qt, kd, vt, lq1, lk1, lq2, lk2, g_subln.reshape(2 * HEAD_DIM, 1),
      dil_qkv, dil_qkv, dil_qkv, *biases)


def _post_kernel(x_ref, od_ref, og_ref, p_ref, wo_ref, gm_ref, wu_ref, wd_ref,
                 gp_ref, wg_ref, wp_ref, gf_ref, o_ref):
    half_rows = x_ref.shape[0] // POST_HALVES
    halves = [pl.ds(i * half_rows, half_rows) for i in range(POST_HALVES)]

    h = []
    for sl in halves:
        mixed = jnp.concatenate([od_ref[sl, :], og_ref[sl, :]], axis=-1)
        h.append(x_ref[sl, :] + jnp.dot(mixed, wo_ref[...], preferred_element_type=F32))

    for i in range(POST_HALVES):
        hn = _rms(h[i], gm_ref[...], NORM_EPS).astype(BF16)
        for c in range(D_FF // FF_CHUNK):
            u = jnp.dot(hn, wu_ref[:, c * FF_CHUNK:(c + 1) * FF_CHUNK], preferred_element_type=F32)
            u = jnp.square(jnp.maximum(u, 0.0)).astype(BF16)
            h[i] = h[i] + jnp.dot(u, wd_ref[c * FF_CHUNK:(c + 1) * FF_CHUNK, :],
                                  preferred_element_type=F32)

    for i, sl in enumerate(halves):
        hn = _rms(h[i], gp_ref[...], NORM_EPS).astype(BF16)
        gate = jax.nn.sigmoid(jnp.dot(hn, wg_ref[...], preferred_element_type=F32))
        emb = jnp.dot(p_ref[sl, :].astype(BF16), wp_ref[...], preferred_element_type=F32)
        o_ref[sl, :] = _rms(h[i] + gate * emb, gf_ref[...], NORM_EPS)


def _post(x2, od2, og2, p2, w_out, g_mlp, w_up, w_down, g_ple, w_gate, w_ple, g_final):
    N, D = x2.shape
    rows = POST_ROWS
    tok = lambda width: pl.BlockSpec((rows, width), lambda i: (i, 0))
    whole = lambda a: pl.BlockSpec(a.shape, lambda i: (0, 0), pipeline_mode=pl.Buffered(1))
    return pl.pallas_call(
        _post_kernel,
        grid=(N // rows,),
        in_specs=[
            tok(D), tok(D_DIFF), tok(D_DIL), tok(D_PLE),
            whole(w_out), whole(g_mlp), whole(w_up), whole(w_down),
            whole(g_ple), whole(w_gate), whole(w_ple), whole(g_final),
        ],
        out_specs=tok(D),
        out_shape=jax.ShapeDtypeStruct((N, D), F32),
        compiler_params=pltpu.CompilerParams(
            dimension_semantics=("parallel",),
            vmem_limit_bytes=VMEM_LIMIT_BYTES),
        name="out_mlp_ple",
    )(x2, od2, og2, p2, w_out, g_mlp, w_up, w_down, g_ple, w_gate, w_ple, g_final)


def _rope_tables(seq):
    inv = ROPE_THETA ** (-np.arange(0, HEAD_DIM, 2, dtype=np.float64) / HEAD_DIM)
    ang = np.arange(seq, dtype=np.float64)[:, None] * inv[None, :]
    cos, sin = np.cos(ang), np.sin(ang)
    reps = LANES // HEAD_DIM
    cos_t = np.tile(np.concatenate([cos, cos], axis=-1), (1, reps))
    sin_t = np.tile(np.concatenate([-sin, sin], axis=-1), (1, reps))
    return tuple(jnp.asarray(t, dtype=F32) for t in (cos_t, sin_t, cos.T, sin.T))


def kernel(x, p, w_in, w_out, g_mix, lambda_q1, lambda_k1, lambda_q2, lambda_k2, g_subln,
           g_mlp, w_up, w_down, g_ple, w_ple_gate, w_ple_proj, g_final):
    B, S, D = x.shape
    assert x.shape[1:] == (2048, D_MODEL) and w_in.shape == (1, D_MODEL, 2 * D_GROUP)

    later = (w_out[0], w_up[0], w_down[0], w_ple_gate[0], w_ple_proj[0])
    (qt, kd, vt, dil_qkv), (wo, wu, wd, wg, wp) = _project(x, g_mix, w_in[0], _rope_tables(S), later)
    od, og = _attention(qt, kd, vt, dil_qkv, lambda_q1, lambda_k1, lambda_q2, lambda_k2, g_subln)

    out = _post(
        x.reshape(B * S, D), od.reshape(B * S, D_DIFF), og.reshape(B * S, D_DIL),
        p[0].reshape(B * S, D_PLE),
        wo, g_mlp, wu, wd, g_ple, wg, wp, g_final.reshape(1, D))
    return out.reshape(B, S, D)
```
